```python
import math
import jax, jax.numpy as jnp
from jax import lax
import numpy as np

D_MODEL = 1024
BATCH = 8
SEQ = 4096
DEPTH = 2

HEAD_DIM = 64
ROPE_THETA = 10000.0
EPS = 1e-6
N_MIXERS = 2
DIL_GROUPS = ((128, 1), (512, 4), (2048, 16))
N_DIL_GROUPS = len(DIL_GROUPS)
DIL_HEADS = D_MODEL // HEAD_DIM
DIL_WIDTH = DIL_HEADS * HEAD_DIM
DIL_IN = 3 * N_DIL_GROUPS * DIL_WIDTH + DIL_WIDTH
DIFF_HEADS = D_MODEL // HEAD_DIM // 2
DIFF_QK_DIM = HEAD_DIM
DIFF_V_DIM = 2 * HEAD_DIM
DIFF_WIDTH = DIFF_HEADS * DIFF_V_DIM
DIFF_QK_WIDTH = 2 * DIFF_HEADS * DIFF_QK_DIM
DIFF_IN = 2 * DIFF_QK_WIDTH + 2 * DIFF_WIDTH
Q_BLOCK = 128
N_DIL_LAYERS = (DEPTH + 1) // 2
N_DIFF_LAYERS = DEPTH // 2

kernel_name = "hybrid_dilated_diff_attention_trunk"


def rmsnorm(x, g):
    x32 = x.astype(jnp.float32)
    y = x32 * lax.rsqrt(jnp.mean(x32 * x32, axis=-1, keepdims=True) + EPS)
    return y.astype(x.dtype) * g


def rope(x, pos):
    dh = x.shape[-1]
    freqs = ROPE_THETA ** (-jnp.arange(0, dh, 2, dtype=jnp.float32) / dh)
    ang = pos.astype(jnp.float32)[:, None] * freqs[None, :]
    cos = jnp.cos(ang)[None, :, None, :].astype(x.dtype)
    sin = jnp.sin(ang)[None, :, None, :].astype(x.dtype)
    x1, x2 = x[..., : dh // 2], x[..., dh // 2:]
    return jnp.concatenate([x1 * cos - x2 * sin, x2 * cos + x1 * sin], axis=-1)


def dilated_window_attention(q, k, v, window, dilation):
    B, S, H, Dh = q.shape
    w = window // dilation
    L = S // dilation
    nb = -(-L // w)
    Lp = nb * w

    def to_sub(t):
        t = t.reshape(B, L, dilation, H, Dh).transpose(0, 2, 1, 3, 4)
        t = jnp.pad(t, ((0, 0), (0, 0), (0, Lp - L), (0, 0), (0, 0)))
        return t.reshape(B, dilation, nb, w, H, Dh)

    qs, ks, vs = to_sub(q), to_sub(k), to_sub(v)

    def with_prev(t):
        prev = jnp.pad(t, ((0, 0), (0, 0), (1, 0), (0, 0), (0, 0), (0, 0)))[:, :, :-1]
        return jnp.concatenate([prev, t], axis=3)

    kb, vb = with_prev(ks), with_prev(vs)
    scores = jnp.einsum('brnqhd,brnkhd->brnhqk', qs, kb).astype(jnp.float32) * (Dh ** -0.5)
    qi = jnp.arange(w)[:, None]
    kj = jnp.arange(2 * w)[None, :]
    dist = w + qi - kj
    kpos = jnp.arange(nb)[:, None, None] * w + kj[None] - w
    mask = (dist >= 0)[None] & (dist <= w)[None] & (kpos >= 0)
    scores = jnp.where(mask[None, None, :, None], scores, -jnp.inf)
    m = jnp.max(scores, axis=-1, keepdims=True)
    p = jnp.exp(scores - m)
    den = jnp.sum(p, axis=-1)
    out = jnp.einsum('brnhqk,brnkhd->brnhqd', p.astype(v.dtype), vb).astype(jnp.float32)
    out = out / den[..., None]
    lse = m[..., 0] + jnp.log(den)
    out = out.transpose(0, 1, 2, 4, 3, 5).reshape(B, dilation, Lp, H, Dh)[:, :, :L]
    out = out.transpose(0, 2, 1, 3, 4).reshape(B, S, H, Dh)
    lse = lse.transpose(0, 1, 2, 4, 3).reshape(B, dilation, Lp, H)[:, :, :L]
    lse = lse.transpose(0, 2, 1, 3).reshape(B, S, H)
    return out, lse


def dilated_mixer(h, w_in, w_out):
    B, S, _ = h.shape
    pos = jnp.arange(S)
    proj = h @ w_in
    n_qkv = 3 * N_DIL_GROUPS * DIL_WIDTH
    qkv = proj[..., :n_qkv].reshape(B, S, 3, N_DIL_GROUPS * DIL_HEADS, HEAD_DIM)
    gate = proj[..., n_qkv:]
    q = rope(qkv[:, :, 0], pos)
    k = rope(qkv[:, :, 1], pos)
    v = qkv[:, :, 2]
    outs, lses = [], []
    for g, (window, dil) in enumerate(DIL_GROUPS):
        sl = slice(g * DIL_HEADS, (g + 1) * DIL_HEADS)
        o, lse = dilated_window_attention(q[:, :, sl], k[:, :, sl], v[:, :, sl], window, dil)
        outs.append(o)
        lses.append(lse)
    alpha = jax.nn.softmax(jnp.stack(lses, axis=0), axis=0)
    o = jnp.sum(alpha[..., None] * jnp.stack(outs, axis=0), axis=0)
    y = o.reshape(B, S, DIL_WIDTH).astype(h.dtype) * jax.nn.silu(gate)
    return y @ w_out


def diff_mixer(h, w_in, lq1, lk1, lq2, lk2, subln, w_out, lambda_init):
    B, S, _ = h.shape
    pos = jnp.arange(S)
    proj = h @ w_in
    q = proj[..., :DIFF_QK_WIDTH].reshape(B, S, 2 * DIFF_HEADS, DIFF_QK_DIM)
    k = proj[..., DIFF_QK_WIDTH:2 * DIFF_QK_WIDTH].reshape(B, S, 2 * DIFF_HEADS, DIFF_QK_DIM)
    v = proj[..., 2 * DIFF_QK_WIDTH:2 * DIFF_QK_WIDTH + DIFF_WIDTH].reshape(B, S, DIFF_HEADS, DIFF_V_DIM)
    gate = proj[..., 2 * DIFF_QK_WIDTH + DIFF_WIDTH:]
    q = rope(q, pos).reshape(B, S, DIFF_HEADS, 2, DIFF_QK_DIM)
    k = rope(k, pos).reshape(B, S, DIFF_HEADS, 2, DIFF_QK_DIM)
    lam = (jnp.exp(jnp.sum(lq1.astype(jnp.float32) * lk1.astype(jnp.float32)))
           - jnp.exp(jnp.sum(lq2.astype(jnp.float32) * lk2.astype(jnp.float32))) + lambda_init)
    n_qb = S // Q_BLOCK
    qb = q.reshape(B, n_qb, Q_BLOCK, DIFF_HEADS, 2, DIFF_QK_DIM).transpose(1, 0, 2, 3, 4, 5)
    scale = DIFF_QK_DIM ** -0.5
    kpos = jnp.arange(S)

    def block(args):
        qblk, bi = args
        s = jnp.einsum('bqhcd,bkhcd->bchqk', qblk, k).astype(jnp.float32) * scale
        qpos = bi * Q_BLOCK + jnp.arange(Q_BLOCK)
        s = jnp.where(kpos[None, :] <= qpos[:, None], s, -jnp.inf)
        p = jax.nn.softmax(s, axis=-1)
        a = p[:, 0] - lam * p[:, 1]
        return jnp.einsum('bhqk,bkhd->bqhd', a.astype(v.dtype), v)

    o = lax.map(block, (qb, jnp.arange(n_qb)))
    o = o.transpose(1, 0, 2, 3, 4).reshape(B, S, DIFF_HEADS, DIFF_V_DIM)
    o = rmsnorm(o, subln) * (1.0 - lambda_init)
    y = o.reshape(B, S, DIFF_WIDTH) * jax.nn.silu(gate)
    return y @ w_out


def setup_inputs(seed: int = 0) -> dict:
    key = jax.random.key(seed)
    ks = jax.random.split(key, 12)
    f32 = jnp.float32
    x = jax.random.normal(ks[0], (BATCH, SEQ, D_MODEL), f32)
    norm_pre = 1.0 + 0.05 * jax.random.normal(ks[1], (DEPTH, D_MODEL), f32)
    norm_post = 1.0 + 0.05 * jax.random.normal(ks[2], (DEPTH, D_MODEL), f32)
    dil_w_in = jax.random.normal(ks[3], (N_DIL_LAYERS, D_MODEL, DIL_IN), f32) * D_MODEL ** -0.5
    dil_w_out = jax.random.normal(ks[4], (N_DIL_LAYERS, DIL_WIDTH, D_MODEL), f32) * DIL_WIDTH ** -0.5
    diff_w_in = jax.random.normal(ks[5], (N_DIFF_LAYERS, D_MODEL, DIFF_IN), f32) * D_MODEL ** -0.5
    diff_w_out = jax.random.normal(ks[6], (N_DIFF_LAYERS, DIFF_WIDTH, D_MODEL), f32) * DIFF_WIDTH ** -0.5
    diff_lambda_q1 = 0.1 * jax.random.normal(ks[7], (N_DIFF_LAYERS, DIFF_QK_DIM), f32)
    diff_lambda_k1 = 0.1 * jax.random.normal(ks[8], (N_DIFF_LAYERS, DIFF_QK_DIM), f32)
    diff_lambda_q2 = 0.1 * jax.random.normal(ks[9], (N_DIFF_LAYERS, DIFF_QK_DIM), f32)
    diff_lambda_k2 = 0.1 * jax.random.normal(ks[10], (N_DIFF_LAYERS, DIFF_QK_DIM), f32)
    diff_subln = 1.0 + 0.05 * jax.random.normal(ks[11], (N_DIFF_LAYERS, DIFF_V_DIM), f32)
    return {"x": x, "norm_pre": norm_pre, "norm_post": norm_post,
            "dil_w_in": dil_w_in, "dil_w_out": dil_w_out,
            "diff_w_in": diff_w_in, "diff_w_out": diff_w_out,
            "diff_lambda_q1": diff_lambda_q1, "diff_lambda_k1": diff_lambda_k1,
            "diff_lambda_q2": diff_lambda_q2, "diff_lambda_k2": diff_lambda_k2,
            "diff_subln": diff_subln}


def reference(x, norm_pre, norm_post, dil_w_in, dil_w_out, diff_w_in, diff_w_out,
              diff_lambda_q1, diff_lambda_k1, diff_lambda_q2, diff_lambda_k2, diff_subln):
    h = x
    for i in range(DEPTH):
        u = rmsnorm(h, norm_pre[i])
        j = i // N_MIXERS
        if i % N_MIXERS == 0:
            y = dilated_mixer(u, dil_w_in[j], dil_w_out[j])
        else:
            lambda_init = 0.8 - 0.6 * math.exp(-0.3 * i)
            y = diff_mixer(u, diff_w_in[j], diff_lambda_q1[j], diff_lambda_k1[j],
                           diff_lambda_q2[j], diff_lambda_k2[j], diff_subln[j],
                           diff_w_out[j], lambda_init)
        h = h + rmsnorm(y, norm_post[i])
    return h
```

```python
import functools
import math

import jax
import jax.numpy as jnp
from jax import lax
from jax.experimental import pallas as pl
from jax.experimental.pallas import tpu as pltpu

D_MODEL = 1024
HEAD_DIM = 64
ROPE_THETA = 10000.0
EPS = 1e-6
DILATIONS = (1, 4, 16)
WIN = 128
LANES = 128
N_CHUNK = D_MODEL // LANES
ROW_TILE = 2048
NEG = -1e30
VMEM_LIMIT = 52 * 1024 * 1024

f32 = jnp.float32
bf16 = jnp.bfloat16


def _cparams(n_axes):
    return pltpu.CompilerParams(
        dimension_semantics=("arbitrary",) * n_axes, vmem_limit_bytes=VMEM_LIMIT)


def _proj_kernel(x_ref, g_ref, w_ref, cos_ref, sin_ref, o_ref, u_ref, t_ref, *,
                 bn, sec_rope, sec_dil):
    j = pl.program_id(1)
    bm = x_ref.shape[0]
    n_c = bn // LANES
    blocks_per_sec = D_MODEL // bn

    @pl.when(j == 0)
    def _():
        x = x_ref[...]
        ms = jnp.mean(x * x, axis=-1, keepdims=True)
        u_ref[...] = ((x * lax.rsqrt(ms + EPS)) * g_ref[...]).astype(bf16)

    acc = jnp.dot(u_ref[...], w_ref[...], preferred_element_type=f32)
    for c in range(n_c):
        t_ref[c] = acc[:, c * LANES:(c + 1) * LANES]

    lane = lax.broadcasted_iota(jnp.int32, (1, LANES), 1)
    first_half = (lane % HEAD_DIM) < (HEAD_DIM // 2)
    pieces = bm // WIN

    def emit(rope, d):
        rows_per_res = bm // d

        def body(k, carry):
            c = k // pieces
            o0 = pl.multiple_of((k % pieces) * WIN, WIN)
            r = o0 // rows_per_res
            src = (o0 % rows_per_res) * d + r
            rows = pl.ds(src, WIN, stride=d) if d > 1 else pl.ds(src, WIN)
            x = t_ref[c, rows, :]
            if rope:
                partner = jnp.where(first_half,
                                    pltpu.roll(x, LANES - HEAD_DIM // 2, axis=1),
                                    pltpu.roll(x, HEAD_DIM // 2, axis=1))
                x = x * cos_ref[rows, :] + partner * sin_ref[rows, :]
            o_ref[c, pl.ds(o0, WIN), :] = x.astype(bf16)
            return carry

        lax.fori_loop(0, n_c * pieces, body, 0)

    sec = j // blocks_per_sec
    combos = sorted(set(zip(sec_rope, sec_dil)))
    for rope, d in combos:
        hit = None
        for s, (rp, dd) in enumerate(zip(sec_rope, sec_dil)):
            if (rp, dd) == (rope, d):
                hit = (sec == s) if hit is None else (hit | (sec == s))
        pl.when(hit)(functools.partial(emit, rope, d))


def _in_proj(x2d, gain, w_bf16, cos_t, sin_t, batch, seq, sec_rope, sec_dil, bn=512):
    m, d_in = x2d.shape
    n_out = w_bf16.shape[1]
    bm = ROW_TILE
    tiles_per_seq = seq // bm
    kern = functools.partial(_proj_kernel, bn=bn, sec_rope=sec_rope, sec_dil=sec_dil)
    return pl.pallas_call(
        kern,
        grid=(m // bm, n_out // bn),
        in_specs=[
            pl.BlockSpec((bm, d_in), lambda i, j: (i, 0)),
            pl.BlockSpec((1, d_in), lambda i, j: (0, 0)),
            pl.BlockSpec((d_in, bn), lambda i, j: (0, j)),
            pl.BlockSpec((bm, LANES), lambda i, j: (i % tiles_per_seq, 0)),
            pl.BlockSpec((bm, LANES), lambda i, j: (i % tiles_per_seq, 0)),
        ],
        out_specs=pl.BlockSpec((None, bn // LANES, bm, LANES),
                               lambda i, j: (i // tiles_per_seq, j, i % tiles_per_seq, 0)),
        out_shape=jax.ShapeDtypeStruct((batch, n_out // LANES, seq, LANES), bf16),
        scratch_shapes=[pltpu.VMEM((bm, d_in), bf16),
                        pltpu.VMEM((bn // LANES, bm, LANES), f32)],
        compiler_params=_cparams(2),
        name="in_proj",
    )(x2d, gain, w_bf16, cos_t, sin_t)


def _dil_attn_kernel(bias_ref, q0, k0, v0, q1, k1, v1, q2, k2, v2, gate_ref, y_ref,
                     o_sc, lse_sc):
    seq = y_ref.shape[0]
    n_blk = seq // WIN
    blk_per_tile = ROW_TILE // WIN
    lane = lax.broadcasted_iota(jnp.int32, (1, LANES), 1)
    low = lane < HEAD_DIM
    scale = HEAD_DIM ** -0.5
    sel0 = jnp.where(low, scale, 0.0).astype(bf16)
    sel1 = jnp.where(low, 0.0, scale).astype(bf16)
    ones = jnp.ones((2 * WIN, LANES), bf16)
    groups = ((q0, k0, v0), (q1, k1, v1), (q2, k2, v2))

    def block(blk, carry):
        t = blk // blk_per_tile
        u = blk % blk_per_tile
        row0 = pl.multiple_of(blk * WIN, WIN)
        for g, d in enumerate(DILATIONS):
            q_ref, k_ref, v_ref = groups[g]
            per_res = blk_per_tile // d
            r = u // per_res
            n = u % per_res
            prev = jnp.where(n > 0, blk - 1, blk - blk_per_tile + per_res - 1)
            has_prev = (n > 0) | (t > 0)
            prow0 = pl.multiple_of(jnp.maximum(prev, 0) * WIN, WIN)
            nat0 = t * ROW_TILE + n * WIN * d + r

            q = q_ref[pl.ds(row0, WIN), :]
            qs = jnp.concatenate([q * sel0, q * sel1], axis=0)
            kcat = jnp.concatenate([k_ref[pl.ds(prow0, WIN), :], k_ref[pl.ds(row0, WIN), :]], axis=0)
            vcat = jnp.concatenate([v_ref[pl.ds(prow0, WIN), :], v_ref[pl.ds(row0, WIN), :]], axis=0)
            s = lax.dot_general(qs, kcat, (((1,), (1,)), ((), ())), preferred_element_type=f32)
            s = s + bias_ref[has_prev.astype(jnp.int32)]
            m = jnp.max(s, axis=-1, keepdims=True)
            p = jnp.exp(s - m).astype(bf16)
            vext = jnp.concatenate([vcat, ones], axis=1)
            pv = jnp.dot(p, vext, preferred_element_type=f32)
            num, den = pv[:, :LANES], pv[:, LANES:]
            o = num / den
            lse = m + jnp.log(den)
            o = jnp.where(low, o[:WIN], o[WIN:])
            lse = jnp.where(low, lse[:WIN], lse[WIN:])
            rows = pl.ds(nat0, WIN, stride=d) if d > 1 else pl.ds(nat0, WIN)
            o_sc[g, rows, :] = o
            lse_sc[g, rows, :] = lse
        return carry

    lax.fori_loop(0, n_blk, block, 0)

    def merge(i, carry):
        rows = pl.ds(pl.multiple_of(i * WIN, WIN), WIN)
        l0, l1, l2 = lse_sc[0, rows, :], lse_sc[1, rows, :], lse_sc[2, rows, :]
        mx = jnp.maximum(jnp.maximum(l0, l1), l2)
        e0, e1, e2 = jnp.exp(l0 - mx), jnp.exp(l1 - mx), jnp.exp(l2 - mx)
        o = (e0 * o_sc[0, rows, :] + e1 * o_sc[1, rows, :] + e2 * o_sc[2, rows, :]) / (e0 + e1 + e2)
        gate = gate_ref[rows, :].astype(f32)
        y_ref[rows, :] = (o * (gate * jax.nn.sigmoid(gate))).astype(bf16)
        return carry

    lax.fori_loop(0, n_blk, merge, 0)


def _dil_attention(proj, bias, batch, seq):
    def slab(c0):
        return pl.BlockSpec((None, None, seq, LANES), lambda b, hp: (b, c0 + hp, 0, 0))

    specs = [pl.BlockSpec((2, 2 * WIN, 2 * WIN), lambda b, hp: (0, 0, 0))]
    args = [bias]
    for g in range(len(DILATIONS)):
        for kind in range(3):
            specs.append(slab((kind * len(DILATIONS) + g) * N_CHUNK))
            args.append(proj)
    specs.append(slab(3 * len(DILATIONS) * N_CHUNK))
    args.append(proj)
    return pl.pallas_call(
        _dil_attn_kernel,
        grid=(batch, N_CHUNK),
        in_specs=specs,
        out_specs=pl.BlockSpec((None, seq, LANES), lambda b, hp: (b, 0, hp)),
        out_shape=jax.ShapeDtypeStruct((batch, seq, D_MODEL), bf16),
        scratch_shapes=[pltpu.VMEM((len(DILATIONS), seq, LANES), f32),
                        pltpu.VMEM((len(DILATIONS), seq, LANES), f32)],
        compiler_params=_cparams(2),
        name="dilated_attention",
    )(*args)


def _diff_attn_kernel(lq1_ref, lk1_ref, lq2_ref, lk2_ref, subln_ref, bias_ref,
                      q_ref, k_ref, v_ref, gate_ref, y_ref, acc_sc, m_sc, *,
                      tq, tk, lambda_init):
    seq = y_ref.shape[0]
    lane = lax.broadcasted_iota(jnp.int32, (1, LANES), 1)
    low = lane < HEAD_DIM
    scale = HEAD_DIM ** -0.5
    sel0 = jnp.where(low, scale, 0.0).astype(bf16)
    sel1 = jnp.where(low, 0.0, scale).astype(bf16)
    ones = jnp.ones((tk, LANES), bf16)
    lam = (jnp.exp(jnp.sum(lq1_ref[...] * lk1_ref[...], axis=-1, keepdims=True))
           - jnp.exp(jnp.sum(lq2_ref[...] * lk2_ref[...], axis=-1, keepdims=True)) + lambda_init)
    ratio = tk // tq

    def q_tile(qi, carry):
        q0 = pl.multiple_of(qi * tq, tq)
        q = q_ref[pl.ds(q0, tq), :]
        qs = jnp.concatenate([q * sel0, q * sel1], axis=0)
        m_sc[...] = jnp.full(m_sc.shape, NEG, f32)
        acc_sc[...] = jnp.zeros(acc_sc.shape, f32)

        def update(kj, bias):
            k0 = pl.multiple_of(kj * tk, tk)
            s = lax.dot_general(qs, k_ref[pl.ds(k0, tk), :], (((1,), (1,)), ((), ())),
                                preferred_element_type=f32)
            if bias is not None:
                s = s + bias
            m_old = m_sc[...]
            m_new = jnp.maximum(m_old, jnp.max(s, axis=-1, keepdims=True))
            alpha = jnp.exp(m_old - m_new)
            p = jnp.exp(s - jnp.tile(m_new, (1, tk // LANES))).astype(bf16)
            vext = jnp.concatenate([v_ref[pl.ds(k0, tk), :], ones], axis=1)
            acc_sc[...] = (jnp.tile(alpha, (1, 2)) * acc_sc[...]
                           + jnp.dot(p, vext, preferred_element_type=f32))
            m_sc[...] = m_new

        n_full = qi // ratio

        def full(kj, c):
            update(kj, None)
            return c

        lax.fori_loop(0, n_full, full, 0)
        bias = jnp.tile(bias_ref[qi % ratio], (2, 1))
        update(n_full, bias)

        acc = acc_sc[...]
        o = acc[:, :LANES] / acc[:, LANES:]
        o = o[:tq] - lam * o[tq:]
        o = o * lax.rsqrt(jnp.mean(o * o, axis=-1, keepdims=True) + EPS)
        o = o * subln_ref[...] * (1.0 - lambda_init)
        gate = gate_ref[pl.ds(q0, tq), :].astype(f32)
        y_ref[pl.ds(q0, tq), :] = (o * (gate * jax.nn.sigmoid(gate))).astype(bf16)
        return carry

    lax.fori_loop(0, seq // tq, q_tile, 0)


def _diff_attention(proj, bias, lq1, lk1, lq2, lk2, subln, batch, seq, lambda_init, tq, tk):
    def slab(c0):
        return pl.BlockSpec((None, None, seq, LANES), lambda b, h: (b, c0 + h, 0, 0))

    def small(a):
        return pl.BlockSpec(a.shape, lambda b, h: (0,) * a.ndim)

    smalls = [lq1, lk1, lq2, lk2, subln, bias]
    kern = functools.partial(_diff_attn_kernel, tq=tq, tk=tk, lambda_init=lambda_init)
    return pl.pallas_call(
        kern,
        grid=(batch, N_CHUNK),
        in_specs=[small(a) for a in smalls] + [slab(0), slab(N_CHUNK), slab(2 * N_CHUNK), slab(3 * N_CHUNK)],
        out_specs=pl.BlockSpec((None, seq, LANES), lambda b, h: (b, 0, h)),
        out_shape=jax.ShapeDtypeStruct((batch, seq, D_MODEL), bf16),
        scratch_shapes=[pltpu.VMEM((2 * tq, 2 * LANES), f32),
                        pltpu.VMEM((2 * tq, LANES), f32)],
        compiler_params=_cparams(2),
        name="diff_attention",
    )(*smalls, proj, proj, proj, proj)


def _out_proj_kernel(y_ref, w_ref, g_ref, h_ref, o_ref):
    z = jnp.dot(y_ref[...], w_ref[...], preferred_element_type=f32)
    z = z * lax.rsqrt(jnp.mean(z * z, axis=-1, keepdims=True) + EPS)
    o_ref[...] = h_ref[...] + z * g_ref[...]


def _out_proj(y2d, w_bf16, gain, h2d, bm=1024):
    m, d = h2d.shape
    return pl.pallas_call(
        _out_proj_kernel,
        grid=(m // bm,),
        in_specs=[
            pl.BlockSpec((bm, d), lambda i: (i, 0)),
            pl.BlockSpec((d, d), lambda i: (0, 0)),
            pl.BlockSpec((1, d), lambda i: (0, 0)),
            pl.BlockSpec((bm, d), lambda i: (i, 0)),
        ],
        out_specs=pl.BlockSpec((bm, d), lambda i: (i, 0)),
        out_shape=jax.ShapeDtypeStruct((m, d), f32),
        compiler_params=_cparams(1),
        name="out_proj",
    )(y2d, w_bf16, gain, h2d)


def _rope_tables(seq):
    half = HEAD_DIM // 2
    freqs = ROPE_THETA ** (-jnp.arange(0, HEAD_DIM, 2, dtype=f32) / HEAD_DIM)
    ang = jnp.arange(seq).astype(f32)[:, None] * freqs[None, :]
    cos, sin = jnp.cos(ang), jnp.sin(ang)
    cos_t = jnp.tile(cos, (1, LANES // half))
    sin_t = jnp.tile(jnp.concatenate([-sin, sin], axis=1), (1, LANES // HEAD_DIM))
    return cos_t, sin_t


def _window_bias():
    qi = jnp.arange(2 * WIN)[:, None] % WIN
    kj = jnp.arange(2 * WIN)[None, :]
    cur = (kj >= WIN) & (kj - WIN <= qi)
    prev = (kj < WIN) & (kj >= qi)
    return jnp.stack([jnp.where(cur, 0.0, NEG), jnp.where(cur | prev, 0.0, NEG)]).astype(f32)


def _causal_bias(tq, tk):
    qi = jnp.arange(tq)[:, None]
    kj = jnp.arange(tk)[None, :]
    return jnp.stack([jnp.where(kj <= qi + o * tq, 0.0, NEG) for o in range(tk // tq)]).astype(f32)


def kernel(x, norm_pre, norm_post, dil_w_in, dil_w_out, diff_w_in, diff_w_out, diff_lambda_q1, diff_lambda_k1, diff_lambda_q2, diff_lambda_k2, diff_subln):
    batch, seq, d_model = x.shape
    assert d_model == D_MODEL and seq % ROW_TILE == 0
    n_groups = len(DILATIONS)
    cos_t, sin_t = _rope_tables(seq)
    h = x.reshape(batch * seq, d_model)

    sec_rope = (True,) * (2 * n_groups) + (False,) * (n_groups + 1)
    sec_dil = DILATIONS * 3 + (1,)
    proj = _in_proj(h, norm_pre[0][None, :], dil_w_in[0].astype(bf16), cos_t, sin_t,
                    batch, seq, sec_rope, sec_dil)
    y = _dil_attention(proj, _window_bias(), batch, seq)
    h = _out_proj(y.reshape(batch * seq, d_model), dil_w_out[0].astype(bf16), norm_post[0][None, :], h)

    lambda_init = 0.8 - 0.6 * math.exp(-0.3 * 1)
    tq, tk = 256, 512
    proj = _in_proj(h, norm_pre[1][None, :], diff_w_in[0].astype(bf16), cos_t, sin_t,
                    batch, seq, (True, True, False, False), (1, 1, 1, 1))
    y = _diff_attention(proj, _causal_bias(tq, tk),
                        diff_lambda_q1[0][None, :], diff_lambda_k1[0][None, :],
                        diff_lambda_q2[0][None, :], diff_lambda_k2[0][None, :],
                        diff_subln[0][None, :], batch, seq, lambda_init, tq, tk)
    h = _out_proj(y.reshape(batch * seq, d_model), diff_w_out[0].astype(bf16), norm_post[1][None, :], h)
    return h.reshape(batch, seq, d_model)
```

```python
import functools
import math

import jax
import jax.numpy as jnp
from jax import lax
from jax.experimental import pallas as pl
from jax.experimental.pallas import tpu as pltpu

D_MODEL = 1024
HEAD_DIM = 64
ROPE_THETA = 10000.0
EPS = 1e-6
DILATIONS = (1, 4, 16)
WIN = 128
LANES = 128
N_CHUNK = D_MODEL // LANES
ROW_TILE = 2048
BLK_PER_TILE = ROW_TILE // WIN
NEG = -1e30
VMEM_LIMIT = 52 * 1024 * 1024

f32 = jnp.float32
bf16 = jnp.bfloat16


def _cparams(n_axes):
    return pltpu.CompilerParams(
        dimension_semantics=("arbitrary",) * n_axes, vmem_limit_bytes=VMEM_LIMIT)


def _prenorm_kernel(*refs, dils):
    x_refs, (g_ref, o_ref, rs_sc) = refs[:N_CHUNK], refs[N_CHUNK:]
    k = pl.program_id(1)
    d_model = N_CHUNK * LANES
    sub = 2 * WIN

    @pl.when(k == 0)
    def _():
        def scale(i, carry):
            rows = pl.ds(pl.multiple_of(i * sub, sub), sub)
            ss = jnp.zeros((sub, 1), f32)
            for x_ref in x_refs:
                x = x_ref[rows, :]
                ss = ss + jnp.sum(x * x, axis=-1, keepdims=True)
            rs_sc[rows, :] = jnp.broadcast_to(lax.rsqrt(ss * (1.0 / d_model) + EPS), (sub, LANES))
            return carry

        lax.fori_loop(0, ROW_TILE // sub, scale, 0)

    def emit(d):
        per_res = BLK_PER_TILE // d

        def body(s, carry):
            src = (s % per_res) * WIN * d + s // per_res
            rows = pl.ds(src, WIN, stride=d) if d > 1 else pl.ds(pl.multiple_of(src, WIN), WIN)
            rs = rs_sc[rows, :]
            o0 = pl.multiple_of(s * WIN, WIN)
            for c, x_ref in enumerate(x_refs):
                cols = slice(c * LANES, (c + 1) * LANES)
                o_ref[pl.ds(o0, WIN), cols] = ((x_ref[rows, :] * rs) * g_ref[:, cols]).astype(bf16)
            return carry

        lax.fori_loop(0, BLK_PER_TILE, body, 0)

    for kk, d in enumerate(dils):
        pl.when(k == kk)(functools.partial(emit, d))


def _prenorm(x2d, gain, dils):
    m, d_model = x2d.shape
    chunk = lambda c: pl.BlockSpec((ROW_TILE, LANES), lambda t, k: (t, c))
    return pl.pallas_call(
        functools.partial(_prenorm_kernel, dils=dils),
        grid=(m // ROW_TILE, len(dils)),
        in_specs=[chunk(c) for c in range(N_CHUNK)] + [pl.BlockSpec((1, d_model), lambda t, k: (0, 0))],
        out_specs=pl.BlockSpec((None, ROW_TILE, d_model), lambda t, k: (k, t, 0)),
        out_shape=jax.ShapeDtypeStruct((len(dils), m, d_model), bf16),
        scratch_shapes=[pltpu.VMEM((ROW_TILE, LANES), f32)],
        compiler_params=_cparams(2),
        name="prenorm",
    )(*([x2d] * N_CHUNK), gain)


def _proj_kernel(u_ref, w_ref, cos_ref, sin_ref, o_ref, *, rope_secs):
    j = pl.program_id(1)
    n_c = o_ref.shape[0]
    lane = lax.broadcasted_iota(jnp.int32, (1, LANES), 1)
    first_half = (lane % HEAD_DIM) < (HEAD_DIM // 2)
    is_rope = functools.reduce(jnp.logical_or, [j == s for s in rope_secs])

    @pl.when(is_rope)
    def _():
        acc = jnp.dot(u_ref[...], w_ref[...], preferred_element_type=f32)
        cos, sin = cos_ref[...], sin_ref[...]
        for c in range(n_c):
            x = acc[:, c * LANES:(c + 1) * LANES]
            partner = jnp.where(first_half,
                                pltpu.roll(x, LANES - HEAD_DIM // 2, axis=1),
                                pltpu.roll(x, HEAD_DIM // 2, axis=1))
            o_ref[c] = (x * cos + partner * sin).astype(bf16)

    @pl.when(jnp.logical_not(is_rope))
    def _():
        acc = jnp.dot(u_ref[...], w_ref[...], preferred_element_type=f32)
        for c in range(n_c):
            o_ref[c] = acc[:, c * LANES:(c + 1) * LANES].astype(bf16)


def _in_proj(u, w_bf16, cos_t, sin_t, batch, seq, sec_dsel, rope_secs, bm=ROW_TILE):
    _, m, d_in = u.shape
    n_out = w_bf16.shape[1]
    bn = D_MODEL
    tiles_per_seq = seq // bm
    bounds = [s for s in range(1, len(sec_dsel)) if sec_dsel[s] != sec_dsel[s - 1]]

    def dsel(j):
        return sum([(j >= b).astype(jnp.int32) for b in bounds], jnp.int32(0)) if bounds else 0

    kern = functools.partial(_proj_kernel, rope_secs=rope_secs)
    return pl.pallas_call(
        kern,
        grid=(m // bm, n_out // bn),
        in_specs=[
            pl.BlockSpec((None, bm, d_in), lambda i, j: (dsel(j), i, 0)),
            pl.BlockSpec((d_in, bn), lambda i, j: (0, j)),
            pl.BlockSpec((None, bm, LANES), lambda i, j: (dsel(j), i % tiles_per_seq, 0)),
            pl.BlockSpec((None, bm, LANES), lambda i, j: (dsel(j), i % tiles_per_seq, 0)),
        ],
        out_specs=pl.BlockSpec((None, bn // LANES, bm, LANES),
                               lambda i, j: (i // tiles_per_seq, j, i % tiles_per_seq, 0)),
        out_shape=jax.ShapeDtypeStruct((batch, n_out // LANES, seq, LANES), bf16),
        compiler_params=_cparams(2),
        name="in_proj",
    )(u, w_bf16, cos_t, sin_t)


def _dil_attn_kernel(bias_ref, q0, k0, v0, q1, k1, v1, q2, k2, v2, gate_ref, y_ref,
                     o_sc, lse_sc):
    seq = y_ref.shape[0]
    n_blk = seq // WIN
    lane = lax.broadcasted_iota(jnp.int32, (1, LANES), 1)
    low = lane < HEAD_DIM
    scale = HEAD_DIM ** -0.5
    sel0 = jnp.where(low, scale, 0.0).astype(bf16)
    sel1 = jnp.where(low, 0.0, scale).astype(bf16)
    ones = jnp.ones((2 * WIN, LANES), bf16)
    groups = ((q0, k0, v0), (q1, k1, v1), (q2, k2, v2))

    def block(blk, carry):
        t = blk // BLK_PER_TILE
        u = blk % BLK_PER_TILE
        row0 = pl.multiple_of(blk * WIN, WIN)
        for g, d in enumerate(DILATIONS):
            q_ref, k_ref, v_ref = groups[g]
            per_res = BLK_PER_TILE // d
            r = u // per_res
            n = u % per_res
            prev = jnp.where(n > 0, blk - 1, blk - BLK_PER_TILE + per_res - 1)
            has_prev = (n > 0) | (t > 0)
            prow0 = pl.multiple_of(jnp.maximum(prev, 0) * WIN, WIN)
            nat0 = t * ROW_TILE + n * WIN * d + r

            q = q_ref[pl.ds(row0, WIN), :]
            qs = jnp.concatenate([q * sel0, q * sel1], axis=0)
            kcat = jnp.concatenate([k_ref[pl.ds(prow0, WIN), :], k_ref[pl.ds(row0, WIN), :]], axis=0)
            vcat = jnp.concatenate([v_ref[pl.ds(prow0, WIN), :], v_ref[pl.ds(row0, WIN), :]], axis=0)
            s = lax.dot_general(qs, kcat, (((1,), (1,)), ((), ())), preferred_element_type=f32)
            s = s + bias_ref[has_prev.astype(jnp.int32)]
            m = jnp.max(s, axis=-1, keepdims=True)
            p = jnp.exp(s - m).astype(bf16)
            vext = jnp.concatenate([vcat, ones], axis=1)
            pv = jnp.dot(p, vext, preferred_element_type=f32)
            num, den = pv[:, :LANES], pv[:, LANES:]
            o = num / den
            lse = m + jnp.log(den)
            o = jnp.where(low, o[:WIN], o[WIN:])
            lse = jnp.where(low, lse[:WIN], lse[WIN:])
            rows = pl.ds(nat0, WIN, stride=d) if d > 1 else pl.ds(nat0, WIN)
            o_sc[g, rows, :] = o
            lse_sc[g, rows, :] = lse
        return carry

    lax.fori_loop(0, n_blk, block, 0)

    def merge(i, carry):
        rows = pl.ds(pl.multiple_of(i * WIN, WIN), WIN)
        l0, l1, l2 = lse_sc[0, rows, :], lse_sc[1, rows, :], lse_sc[2, rows, :]
        mx = jnp.maximum(jnp.maximum(l0, l1), l2)
        e0, e1, e2 = jnp.exp(l0 - mx), jnp.exp(l1 - mx), jnp.exp(l2 - mx)
        o = (e0 * o_sc[0, rows, :] + e1 * o_sc[1, rows, :] + e2 * o_sc[2, rows, :]) / (e0 + e1 + e2)
        gate = gate_ref[rows, :].astype(f32)
        y_ref[rows, :] = (o * (gate * jax.nn.sigmoid(gate))).astype(bf16)
        return carry

    lax.fori_loop(0, n_blk, merge, 0)


def _dil_attention(proj, bias, batch, seq, sec_of):
    def slab(sec):
        return pl.BlockSpec((None, None, seq, LANES), lambda b, hp: (b, sec * N_CHUNK + hp, 0, 0))

    specs = [pl.BlockSpec((2, 2 * WIN, 2 * WIN), lambda b, hp: (0, 0, 0))]
    args = [bias]
    for g in range(len(DILATIONS)):
        for kind in "qkv":
            specs.append(slab(sec_of[kind + str(g)]))
            args.append(proj)
    specs.append(slab(sec_of["gate"]))
    args.append(proj)
    return pl.pallas_call(
        _dil_attn_kernel,
        grid=(batch, N_CHUNK),
        in_specs=specs,
        out_specs=pl.BlockSpec((None, seq, LANES), lambda b, hp: (b, 0, hp)),
        out_shape=jax.ShapeDtypeStruct((batch, seq, D_MODEL), bf16),
        scratch_shapes=[pltpu.VMEM((len(DILATIONS), seq, LANES), f32),
                        pltpu.VMEM((len(DILATIONS), seq, LANES), f32)],
        compiler_params=_cparams(2),
        name="dilated_attention",
    )(*args)


TAB_QI, TAB_KJ, TAB_BIAS, TAB_FIRST, TAB_LAST = range(5)


def _causal_tiles(seq, tq, tk):
    rows = []
    for qi in range(seq // tq):
        n_full = (qi * tq) // tk
        for kj in range(n_full + 1):
            diag = kj == n_full
            off = (qi * tq - kj * tk) // tq
            rows.append((qi, kj, 1 + off if diag else 0, int(kj == 0), int(diag)))
    return list(zip(*rows))


def _diff_attn_kernel(tab_ref, lq1_ref, lk1_ref, lq2_ref, lk2_ref, subln_ref, bias_ref,
                      q_ref, k_ref, v_ref, gate_ref, y_ref, acc_sc, m_sc, al_sc, p_sc, s_sc, *,
                      tq, tk, n_tiles, unroll, lambda_init):
    lane = lax.broadcasted_iota(jnp.int32, (1, LANES), 1)
    low = lane < HEAD_DIM
    scale = HEAD_DIM ** -0.5
    sel0 = jnp.where(low, scale, 0.0).astype(bf16)
    sel1 = jnp.where(low, 0.0, scale).astype(bf16)
    ones = jnp.ones((tk, LANES), bf16)
    lam = (jnp.exp(jnp.sum(lq1_ref[...] * lk1_ref[...], axis=-1, keepdims=True))
           - jnp.exp(jnp.sum(lq2_ref[...] * lk2_ref[...], axis=-1, keepdims=True)) + lambda_init)

    def scores(t):
        q0 = pl.multiple_of(tab_ref[TAB_QI, t] * tq, tq)
        k0 = pl.multiple_of(tab_ref[TAB_KJ, t] * tk, tk)
        q = q_ref[pl.ds(q0, tq), :]
        qs = jnp.concatenate([q * sel0, q * sel1], axis=0)
        return lax.dot_general(qs, k_ref[pl.ds(k0, tk), :], (((1,), (1,)), ((), ())),
                               preferred_element_type=f32)

    def accumulate(t, slot):
        qi = tab_ref[TAB_QI, t]
        k0 = pl.multiple_of(tab_ref[TAB_KJ, t] * tk, tk)
        vext = jnp.concatenate([v_ref[pl.ds(k0, tk), :], ones], axis=1)
        acc_sc[qi] = (jnp.tile(al_sc[slot], (1, 2)) * acc_sc[qi]
                      + jnp.dot(p_sc[slot], vext, preferred_element_type=f32))

    @pl.when((pl.program_id(0) == 0) & (pl.program_id(1) == 0))
    def _():
        acc_sc[...] = jnp.zeros(acc_sc.shape, f32)

    al_sc[1] = jnp.zeros(al_sc.shape[1:], f32)
    p_sc[1] = jnp.zeros(p_sc.shape[1:], bf16)
    s_sc[0] = scores(0)

    def step(t, cur):
        nxt = 1 - cur
        accumulate(jnp.maximum(t - 1, 0), nxt)

        s = s_sc[cur] + jnp.tile(bias_ref[tab_ref[TAB_BIAS, t]], (2, 1))
        m_old = jnp.where(tab_ref[TAB_FIRST, t] == 1, NEG, m_sc[...])
        m_new = jnp.maximum(m_old, jnp.max(s, axis=-1, keepdims=True))
        al_sc[cur] = jnp.exp(m_old - m_new)
        m_sc[...] = m_new
        p_sc[cur] = jnp.exp(s - jnp.tile(m_new, (1, tk // LANES))).astype(bf16)

        s_sc[nxt] = scores(jnp.minimum(t + 1, n_tiles - 1))

    def steps(i, carry):
        for k in range(unroll):
            step(unroll * i + k, k % 2)
        return carry

    assert unroll % 2 == 0 and n_tiles % unroll == 0
    lax.fori_loop(0, n_tiles // unroll, steps, 0)
    accumulate(n_tiles - 1, 1)

    def finalize(qi, carry):
        q0 = pl.multiple_of(qi * tq, tq)
        acc = acc_sc[qi]
        o = acc[:, :LANES] / acc[:, LANES:]
        o = o[:tq] - lam * o[tq:]
        o = o * lax.rsqrt(jnp.mean(o * o, axis=-1, keepdims=True) + EPS)
        o = o * subln_ref[...] * (1.0 - lambda_init)
        gate = gate_ref[pl.ds(q0, tq), :].astype(f32)
        y_ref[pl.ds(q0, tq), :] = (o * (gate * jax.nn.sigmoid(gate))).astype(bf16)
        return carry

    lax.fori_loop(0, y_ref.shape[0] // tq, finalize, 0)


def _diff_attention(proj, lq1, lk1, lq2, lk2, subln, batch, seq, lambda_init, tq, tk):
    def slab(c0):
        return pl.BlockSpec((None, None, seq, LANES), lambda b, h: (b, c0 + h, 0, 0))

    def small(a):
        return pl.BlockSpec(a.shape, lambda b, h: (0,) * a.ndim)

    table = jnp.array(_causal_tiles(seq, tq, tk), jnp.int32)
    n_tiles = table.shape[1]
    qi = jnp.arange(tq)[:, None]
    kj = jnp.arange(tk)[None, :]
    bias = jnp.stack([jnp.zeros((tq, tk), f32)]
                     + [jnp.where(kj <= qi + o * tq, 0.0, NEG).astype(f32) for o in range(tk // tq)])
    smalls = [lq1, lk1, lq2, lk2, subln, bias]
    kern = functools.partial(_diff_attn_kernel, tq=tq, tk=tk, n_tiles=n_tiles, unroll=8,
                             lambda_init=lambda_init)
    return pl.pallas_call(
        kern,
        grid=(batch, N_CHUNK),
        in_specs=([pl.BlockSpec(memory_space=pltpu.SMEM)] + [small(a) for a in smalls]
                  + [slab(0), slab(N_CHUNK), slab(2 * N_CHUNK), slab(3 * N_CHUNK)]),
        out_specs=pl.BlockSpec((None, seq, LANES), lambda b, h: (b, 0, h)),
        out_shape=jax.ShapeDtypeStruct((batch, seq, D_MODEL), bf16),
        scratch_shapes=[pltpu.VMEM((seq // tq, 2 * tq, 2 * LANES), f32),
                        pltpu.VMEM((2 * tq, LANES), f32),
                        pltpu.VMEM((2, 2 * tq, LANES), f32),
                        pltpu.VMEM((2, 2 * tq, tk), bf16),
                        pltpu.VMEM((2, 2 * tq, tk), f32)],
        compiler_params=_cparams(2),
        name="diff_attention",
    )(table, *smalls, proj, proj, proj, proj)


def _out_proj_kernel(y_ref, w_ref, g_ref, h_ref, o_ref):
    z = jnp.dot(y_ref[...], w_ref[...], preferred_element_type=f32)
    z = z * lax.rsqrt(jnp.mean(z * z, axis=-1, keepdims=True) + EPS)
    o_ref[...] = h_ref[...] + z * g_ref[...]


def _out_proj(y2d, w_bf16, gain, h2d, bm=1024):
    m, d = h2d.shape
    return pl.pallas_call(
        _out_proj_kernel,
        grid=(m // bm,),
        in_specs=[
            pl.BlockSpec((bm, d), lambda i: (i, 0)),
            pl.BlockSpec((d, d), lambda i: (0, 0)),
            pl.BlockSpec((1, d), lambda i: (0, 0)),
            pl.BlockSpec((bm, d), lambda i: (i, 0)),
        ],
        out_specs=pl.BlockSpec((bm, d), lambda i: (i, 0)),
        out_shape=jax.ShapeDtypeStruct((m, d), f32),
        compiler_params=_cparams(1),
        name="out_proj",
    )(y2d, w_bf16, gain, h2d)


def _rope_tables(seq, dils):
    half = HEAD_DIM // 2
    freqs = ROPE_THETA ** (-jnp.arange(0, HEAD_DIM, 2, dtype=f32) / HEAD_DIM)
    ang = jnp.arange(seq).astype(f32)[:, None] * freqs[None, :]
    cos, sin = jnp.cos(ang), jnp.sin(ang)
    cos_t = jnp.tile(cos, (1, LANES // half))
    sin_t = jnp.tile(jnp.concatenate([-sin, sin], axis=1), (1, LANES // HEAD_DIM))

    def reorder(t, d):
        return t.reshape(seq // ROW_TILE, ROW_TILE // d, d, LANES).transpose(0, 2, 1, 3).reshape(seq, LANES)

    return (jnp.stack([reorder(cos_t, d) for d in dils]), jnp.stack([reorder(sin_t, d) for d in dils]))


def _window_bias():
    qi = jnp.arange(2 * WIN)[:, None] % WIN
    kj = jnp.arange(2 * WIN)[None, :]
    cur = (kj >= WIN) & (kj - WIN <= qi)
    prev = (kj < WIN) & (kj >= qi)
    return jnp.stack([jnp.where(cur, 0.0, NEG), jnp.where(cur | prev, 0.0, NEG)]).astype(f32)


def kernel(x, norm_pre, norm_post, dil_w_in, dil_w_out, diff_w_in, diff_w_out, diff_lambda_q1, diff_lambda_k1, diff_lambda_q2, diff_lambda_k2, diff_subln):
    batch, seq, d_model = x.shape
    assert d_model == D_MODEL and seq % ROW_TILE == 0
    n_g = len(DILATIONS)
    h = x.reshape(batch * seq, d_model)

    names = [kind + str(g) for kind in "qkv" for g in range(n_g)] + ["gate"]
    order = sorted(range(len(names)), key=lambda s: (0 if names[s] == "gate" else int(names[s][1]), s))
    sec_of = {names[s]: pos for pos, s in enumerate(order)}
    sec_dsel = [0 if names[s] == "gate" else int(names[s][1]) for s in order]
    rope_secs = [pos for pos, s in enumerate(order) if names[s][0] in "qk"]
    w_in = dil_w_in[0].reshape(d_model, len(names), D_MODEL)[:, jnp.array(order)].reshape(d_model, -1)
    cos_t, sin_t = _rope_tables(seq, DILATIONS)
    u = _prenorm(h, norm_pre[0][None, :], DILATIONS)
    proj = _in_proj(u, w_in.astype(bf16), cos_t, sin_t, batch, seq, sec_dsel, rope_secs)
    y = _dil_attention(proj, _window_bias(), batch, seq, sec_of)
    h = _out_proj(y.reshape(batch * seq, d_model), dil_w_out[0].astype(bf16), norm_post[0][None, :], h)

    lambda_init = 0.8 - 0.6 * math.exp(-0.3 * 1)
    tq, tk = 256, 512
    u = _prenorm(h, norm_pre[1][None, :], (1,))
    proj = _in_proj(u, diff_w_in[0].astype(bf16), cos_t[:1], sin_t[:1], batch, seq, [0, 0, 0, 0], [0, 1])
    y = _diff_attention(proj, diff_lambda_q1[0][None, :], diff_lambda_k1[0][None, :],
                        diff_lambda_q2[0][None, :], diff_lambda_k2[0][None, :],
                        diff_subln[0][None, :], batch, seq, lambda_init, tq, tk)
    h = _out_proj(y.reshape(batch * seq, d_model), diff_w_out[0].astype(bf16), norm_post[1][None, :], h)
    return h.reshape(batch, seq, d_model)
```

```python
import functools
import math

import jax
import jax.numpy as jnp
from jax import lax
from jax.experimental import pallas as pl
from jax.experimental.pallas import tpu as pltpu

D_MODEL = 1024
HEAD_DIM = 64
ROPE_THETA = 10000.0
EPS = 1e-6
DILATIONS = (1, 4, 16)
WIN = 128
LANES = 128
N_CHUNK = D_MODEL // LANES
ROW_TILE = 2048
BLK_PER_TILE = ROW_TILE // WIN
NEG = -1e30
VMEM_LIMIT = 52 * 1024 * 1024

f32 = jnp.float32
bf16 = jnp.bfloat16


def _cparams(n_axes):
    return pltpu.CompilerParams(
        dimension_semantics=("arbitrary",) * n_axes, vmem_limit_bytes=VMEM_LIMIT)


def _prenorm_kernel(*refs, dils):
    x_refs, (g_ref, o_ref, rs_sc) = refs[:N_CHUNK], refs[N_CHUNK:]
    k = pl.program_id(1)
    d_model = N_CHUNK * LANES
    sub = 2 * WIN

    @pl.when(k == 0)
    def _():
        def scale(i, carry):
            rows = pl.ds(pl.multiple_of(i * sub, sub), sub)
            ss = jnp.zeros((sub, 1), f32)
            for x_ref in x_refs:
                x = x_ref[rows, :]
                ss = ss + jnp.sum(x * x, axis=-1, keepdims=True)
            rs_sc[rows, :] = jnp.broadcast_to(lax.rsqrt(ss * (1.0 / d_model) + EPS), (sub, LANES))
            return carry

        lax.fori_loop(0, ROW_TILE // sub, scale, 0)

    def emit(d):
        per_res = BLK_PER_TILE // d

        def body(s, carry):
            src = (s % per_res) * WIN * d + s // per_res
            rows = pl.ds(src, WIN, stride=d) if d > 1 else pl.ds(pl.multiple_of(src, WIN), WIN)
            rs = rs_sc[rows, :]
            o0 = pl.multiple_of(s * WIN, WIN)
            for c, x_ref in enumerate(x_refs):
                cols = slice(c * LANES, (c + 1) * LANES)
                o_ref[pl.ds(o0, WIN), cols] = ((x_ref[rows, :] * rs) * g_ref[:, cols]).astype(bf16)
            return carry

        lax.fori_loop(0, BLK_PER_TILE, body, 0)

    for kk, d in enumerate(dils):
        pl.when(k == kk)(functools.partial(emit, d))


def _prenorm(x2d, gain, dils):
    m, d_model = x2d.shape
    chunk = lambda c: pl.BlockSpec((ROW_TILE, LANES), lambda t, k: (t, c))
    return pl.pallas_call(
        functools.partial(_prenorm_kernel, dils=dils),
        grid=(m // ROW_TILE, len(dils)),
        in_specs=[chunk(c) for c in range(N_CHUNK)] + [pl.BlockSpec((1, d_model), lambda t, k: (0, 0))],
        out_specs=pl.BlockSpec((None, ROW_TILE, d_model), lambda t, k: (k, t, 0)),
        out_shape=jax.ShapeDtypeStruct((len(dils), m, d_model), bf16),
        scratch_shapes=[pltpu.VMEM((ROW_TILE, LANES), f32)],
        compiler_params=_cparams(2),
        name="prenorm",
    )(*([x2d] * N_CHUNK), gain)


def _proj_kernel(u_ref, w_ref, cos_ref, sin_ref, o_ref, *, rope_secs):
    j = pl.program_id(1)
    n_c = o_ref.shape[0]
    lane = lax.broadcasted_iota(jnp.int32, (1, LANES), 1)
    first_half = (lane % HEAD_DIM) < (HEAD_DIM // 2)
    is_rope = functools.reduce(jnp.logical_or, [j == s for s in rope_secs])

    @pl.when(is_rope)
    def _():
        cos, sin = cos_ref[...], sin_ref[...]
        half = n_c // 2
        for part in range(2):
            cols = slice(part * half * LANES, (part + 1) * half * LANES)
            acc = jnp.dot(u_ref[...], w_ref[:, cols], preferred_element_type=f32)
            for c in range(half):
                x = acc[:, c * LANES:(c + 1) * LANES]
                xb = x.astype(bf16)
                partner = jnp.where(first_half,
                                    pltpu.roll(xb, LANES - HEAD_DIM // 2, axis=1),
                                    pltpu.roll(xb, HEAD_DIM // 2, axis=1)).astype(f32)
                o_ref[part * half + c] = (x * cos + partner * sin).astype(bf16)

    @pl.when(jnp.logical_not(is_rope))
    def _():
        acc = jnp.dot(u_ref[...], w_ref[...], preferred_element_type=f32)
        for c in range(n_c):
            o_ref[c] = acc[:, c * LANES:(c + 1) * LANES].astype(bf16)


def _in_proj(u, w_bf16, cos_t, sin_t, batch, seq, sec_dsel, rope_secs, bm=ROW_TILE):
    _, m, d_in = u.shape
    n_out = w_bf16.shape[1]
    bn = D_MODEL
    tiles_per_seq = seq // bm
    bounds = [s for s in range(1, len(sec_dsel)) if sec_dsel[s] != sec_dsel[s - 1]]

    def dsel(j):
        return sum([(j >= b).astype(jnp.int32) for b in bounds], jnp.int32(0)) if bounds else 0

    kern = functools.partial(_proj_kernel, rope_secs=rope_secs)
    return pl.pallas_call(
        kern,
        grid=(m // bm, n_out // bn),
        in_specs=[
            pl.BlockSpec((None, bm, d_in), lambda i, j: (dsel(j), i, 0)),
            pl.BlockSpec((d_in, bn), lambda i, j: (0, j)),
            pl.BlockSpec((None, bm, LANES), lambda i, j: (dsel(j), i % tiles_per_seq, 0)),
            pl.BlockSpec((None, bm, LANES), lambda i, j: (dsel(j), i % tiles_per_seq, 0)),
        ],
        out_specs=pl.BlockSpec((None, bn // LANES, bm, LANES),
                               lambda i, j: (i // tiles_per_seq, j, i % tiles_per_seq, 0)),
        out_shape=jax.ShapeDtypeStruct((batch, n_out // LANES, seq, LANES), bf16),
        compiler_params=_cparams(2),
        name="in_proj",
    )(u, w_bf16, cos_t, sin_t)


def _dil_attn_kernel(bias_ref, q0, k0, v0, q1, k1, v1, q2, k2, v2, gate_ref, y_ref,
                     num_sc, den_sc, max_sc, *, unroll):
    seq = y_ref.shape[0]
    n_blk = seq // WIN
    lane = lax.broadcasted_iota(jnp.int32, (1, LANES), 1)
    low = lane < HEAD_DIM
    scale = HEAD_DIM ** -0.5
    sel0 = jnp.where(low, scale, 0.0).astype(bf16)
    sel1 = jnp.where(low, 0.0, scale).astype(bf16)
    ones = jnp.ones((2 * WIN, LANES), bf16)
    groups = ((q0, k0, v0), (q1, k1, v1), (q2, k2, v2))

    def block(blk, carry):
        t = blk // BLK_PER_TILE
        u = blk % BLK_PER_TILE
        row0 = pl.multiple_of(blk * WIN, WIN)
        for g, d in enumerate(DILATIONS):
            q_ref, k_ref, v_ref = groups[g]
            per_res = BLK_PER_TILE // d
            r = u // per_res
            n = u % per_res
            prev = jnp.where(n > 0, blk - 1, blk - BLK_PER_TILE + per_res - 1)
            has_prev = (n > 0) | (t > 0)
            prow0 = pl.multiple_of(jnp.maximum(prev, 0) * WIN, WIN)
            nat0 = t * ROW_TILE + n * WIN * d + r

            q = q_ref[pl.ds(row0, WIN), :]
            qs = jnp.concatenate([q * sel0, q * sel1], axis=0)
            kcat = jnp.concatenate([k_ref[pl.ds(prow0, WIN), :], k_ref[pl.ds(row0, WIN), :]], axis=0)
            vcat = jnp.concatenate([v_ref[pl.ds(prow0, WIN), :], v_ref[pl.ds(row0, WIN), :]], axis=0)
            s = lax.dot_general(qs, kcat, (((1,), (1,)), ((), ())), preferred_element_type=f32)
            s = s + bias_ref[has_prev.astype(jnp.int32)]
            m = jnp.max(s, axis=-1, keepdims=True)
            p = jnp.exp(s - m).astype(bf16)
            vext = jnp.concatenate([vcat, ones], axis=1)
            pv = jnp.dot(p, vext, preferred_element_type=f32)
            rows = pl.ds(nat0, WIN, stride=d) if d > 1 else pl.ds(nat0, WIN)
            num_sc[g, rows, :] = jnp.where(low, pv[:WIN, :LANES], pv[WIN:, :LANES])
            den_sc[g, rows, :] = jnp.where(low, pv[:WIN, LANES:], pv[WIN:, LANES:])
            max_sc[g, rows, :] = jnp.where(low, m[:WIN], m[WIN:])
        return carry

    lax.fori_loop(0, n_blk, block, 0, unroll=unroll)

    def merge(i, carry):
        rows = pl.ds(pl.multiple_of(i * WIN, WIN), WIN)
        m0, m1, m2 = max_sc[0, rows, :], max_sc[1, rows, :], max_sc[2, rows, :]
        mx = jnp.maximum(jnp.maximum(m0, m1), m2)
        e0, e1, e2 = jnp.exp(m0 - mx), jnp.exp(m1 - mx), jnp.exp(m2 - mx)
        num = e0 * num_sc[0, rows, :] + e1 * num_sc[1, rows, :] + e2 * num_sc[2, rows, :]
        den = e0 * den_sc[0, rows, :] + e1 * den_sc[1, rows, :] + e2 * den_sc[2, rows, :]
        gate = gate_ref[rows, :].astype(f32)
        y_ref[rows, :] = ((num / den) * (gate * jax.nn.sigmoid(gate))).astype(bf16)
        return carry

    lax.fori_loop(0, n_blk, merge, 0, unroll=2)


def _dil_attention(proj, bias, batch, seq, sec_of):
    def slab(sec):
        return pl.BlockSpec((None, None, seq, LANES), lambda b, hp: (b, sec * N_CHUNK + hp, 0, 0))

    specs = [pl.BlockSpec((2, 2 * WIN, 2 * WIN), lambda b, hp: (0, 0, 0))]
    args = [bias]
    for g in range(len(DILATIONS)):
        for kind in "qkv":
            specs.append(slab(sec_of[kind + str(g)]))
            args.append(proj)
    specs.append(slab(sec_of["gate"]))
    args.append(proj)
    return pl.pallas_call(
        functools.partial(_dil_attn_kernel, unroll=16),
        grid=(batch, N_CHUNK),
        in_specs=specs,
        out_specs=pl.BlockSpec((None, seq, LANES), lambda b, hp: (b, 0, hp)),
        out_shape=jax.ShapeDtypeStruct((batch, seq, D_MODEL), bf16),
        scratch_shapes=[pltpu.VMEM((len(DILATIONS), seq, LANES), f32)] * 3,
        compiler_params=_cparams(2),
        name="dilated_attention",
    )(*args)


TAB_QI, TAB_KJ, TAB_BIAS, TAB_FIRST, TAB_LAST = range(5)


def _causal_tiles(seq, tq, tk):
    rows = []
    for qi in range(seq // tq):
        n_full = (qi * tq) // tk
        for kj in range(n_full + 1):
            diag = kj == n_full
            off = (qi * tq - kj * tk) // tq
            rows.append((qi, kj, 1 + off if diag else 0, int(kj == 0), int(diag)))
    return list(zip(*rows))


def _diff_attn_kernel(tab_ref, lq1_ref, lk1_ref, lq2_ref, lk2_ref, subln_ref, bias_ref,
                      q_ref, k_ref, v_ref, gate_ref, y_ref, acc_sc, m_sc, al_sc, p_sc, s_sc, *,
                      tq, tk, n_tiles, unroll, lambda_init):
    lane = lax.broadcasted_iota(jnp.int32, (1, LANES), 1)
    low = lane < HEAD_DIM
    scale = HEAD_DIM ** -0.5
    sel0 = jnp.where(low, scale, 0.0).astype(bf16)
    sel1 = jnp.where(low, 0.0, scale).astype(bf16)
    ones = jnp.ones((tk, LANES), bf16)
    lam = (jnp.exp(jnp.sum(lq1_ref[...] * lk1_ref[...], axis=-1, keepdims=True))
           - jnp.exp(jnp.sum(lq2_ref[...] * lk2_ref[...], axis=-1, keepdims=True)) + lambda_init)

    def scores(t):
        q0 = pl.multiple_of(tab_ref[TAB_QI, t] * tq, tq)
        k0 = pl.multiple_of(tab_ref[TAB_KJ, t] * tk, tk)
        q = q_ref[pl.ds(q0, tq), :]
        qs = jnp.concatenate([q * sel0, q * sel1], axis=0)
        return lax.dot_general(qs, k_ref[pl.ds(k0, tk), :], (((1,), (1,)), ((), ())),
                               preferred_element_type=f32)

    def accumulate(t, slot):
        qi = tab_ref[TAB_QI, t]
        k0 = pl.multiple_of(tab_ref[TAB_KJ, t] * tk, tk)
        vext = jnp.concatenate([v_ref[pl.ds(k0, tk), :], ones], axis=1)
        acc_sc[qi] = (jnp.tile(al_sc[slot], (1, 2)) * acc_sc[qi]
                      + jnp.dot(p_sc[slot], vext, preferred_element_type=f32))

    @pl.when((pl.program_id(0) == 0) & (pl.program_id(1) == 0))
    def _():
        acc_sc[...] = jnp.zeros(acc_sc.shape, f32)

    al_sc[1] = jnp.zeros(al_sc.shape[1:], f32)
    p_sc[1] = jnp.zeros(p_sc.shape[1:], bf16)
    s_sc[0] = scores(0)

    def step(t, cur):
        nxt = 1 - cur
        accumulate(jnp.maximum(t - 1, 0), nxt)

        s = s_sc[cur] + jnp.tile(bias_ref[tab_ref[TAB_BIAS, t]], (2, 1))
        m_old = jnp.where(tab_ref[TAB_FIRST, t] == 1, NEG, m_sc[...])
        m_new = jnp.maximum(m_old, jnp.max(s, axis=-1, keepdims=True))
        al_sc[cur] = jnp.exp(m_old - m_new)
        m_sc[...] = m_new
        p_sc[cur] = jnp.exp(s - jnp.tile(m_new, (1, tk // LANES))).astype(bf16)

        s_sc[nxt] = scores(jnp.minimum(t + 1, n_tiles - 1))

    def steps(i, carry):
        for k in range(unroll):
            step(unroll * i + k, k % 2)
        return carry

    assert unroll % 2 == 0 and n_tiles % unroll == 0
    lax.fori_loop(0, n_tiles // unroll, steps, 0)
    accumulate(n_tiles - 1, 1)

    def finalize(qi, carry):
        q0 = pl.multiple_of(qi * tq, tq)
        acc = acc_sc[qi]
        o = acc[:, :LANES] / acc[:, LANES:]
        o = o[:tq] - lam * o[tq:]
        o = o * lax.rsqrt(jnp.mean(o * o, axis=-1, keepdims=True) + EPS)
        o = o * subln_ref[...] * (1.0 - lambda_init)
        gate = gate_ref[pl.ds(q0, tq), :].astype(f32)
        y_ref[pl.ds(q0, tq), :] = (o * (gate * jax.nn.sigmoid(gate))).astype(bf16)
        return carry

    lax.fori_loop(0, y_ref.shape[0] // tq, finalize, 0)


def _diff_attention(proj, lq1, lk1, lq2, lk2, subln, batch, seq, lambda_init, tq, tk):
    def slab(c0):
        return pl.BlockSpec((None, None, seq, LANES), lambda b, h: (b, c0 + h, 0, 0))

    def small(a):
        return pl.BlockSpec(a.shape, lambda b, h: (0,) * a.ndim)

    table = jnp.array(_causal_tiles(seq, tq, tk), jnp.int32)
    n_tiles = table.shape[1]
    qi = jnp.arange(tq)[:, None]
    kj = jnp.arange(tk)[None, :]
    bias = jnp.stack([jnp.zeros((tq, tk), f32)]
                     + [jnp.where(kj <= qi + o * tq, 0.0, NEG).astype(f32) for o in range(tk // tq)])
    smalls = [lq1, lk1, lq2, lk2, subln, bias]
    kern = functools.partial(_diff_attn_kernel, tq=tq, tk=tk, n_tiles=n_tiles, unroll=8,
                             lambda_init=lambda_init)
    return pl.pallas_call(
        kern,
        grid=(batch, N_CHUNK),
        in_specs=([pl.BlockSpec(memory_space=pltpu.SMEM)] + [small(a) for a in smalls]
                  + [slab(0), slab(N_CHUNK), slab(2 * N_CHUNK), slab(3 * N_CHUNK)]),
        out_specs=pl.BlockSpec((None, seq, LANES), lambda b, h: (b, 0, h)),
        out_shape=jax.ShapeDtypeStruct((batch, seq, D_MODEL), bf16),
        scratch_shapes=[pltpu.VMEM((seq // tq, 2 * tq, 2 * LANES), f32),
                        pltpu.VMEM((2 * tq, LANES), f32),
                        pltpu.VMEM((2, 2 * tq, LANES), f32),
                        pltpu.VMEM((2, 2 * tq, tk), bf16),
                        pltpu.VMEM((2, 2 * tq, tk), f32)],
        compiler_params=_cparams(2),
        name="diff_attention",
    )(table, *smalls, proj, proj, proj, proj)


def _out_proj_kernel(y_ref, w_ref, g_ref, h_ref, o_ref):
    z = jnp.dot(y_ref[...], w_ref[...], preferred_element_type=f32)
    z = z * lax.rsqrt(jnp.mean(z * z, axis=-1, keepdims=True) + EPS)
    o_ref[...] = h_ref[...] + z * g_ref[...]


def _out_proj(y2d, w_bf16, gain, h2d, bm=1024):
    m, d = h2d.shape
    return pl.pallas_call(
        _out_proj_kernel,
        grid=(m // bm,),
        in_specs=[
            pl.BlockSpec((bm, d), lambda i: (i, 0)),
            pl.BlockSpec((d, d), lambda i: (0, 0)),
            pl.BlockSpec((1, d), lambda i: (0, 0)),
            pl.BlockSpec((bm, d), lambda i: (i, 0)),
        ],
        out_specs=pl.BlockSpec((bm, d), lambda i: (i, 0)),
        out_shape=jax.ShapeDtypeStruct((m, d), f32),
        compiler_params=_cparams(1),
        name="out_proj",
    )(y2d, w_bf16, gain, h2d)


def _rope_tables(seq, dils):
    half = HEAD_DIM // 2
    freqs = ROPE_THETA ** (-jnp.arange(0, HEAD_DIM, 2, dtype=f32) / HEAD_DIM)
    ang = jnp.arange(seq).astype(f32)[:, None] * freqs[None, :]
    cos, sin = jnp.cos(ang), jnp.sin(ang)
    cos_t = jnp.tile(cos, (1, LANES // half))
    sin_t = jnp.tile(jnp.concatenate([-sin, sin], axis=1), (1, LANES // HEAD_DIM))

    def reorder(t, d):
        return t.reshape(seq // ROW_TILE, ROW_TILE // d, d, LANES).transpose(0, 2, 1, 3).reshape(seq, LANES)

    return (jnp.stack([reorder(cos_t, d) for d in dils]), jnp.stack([reorder(sin_t, d) for d in dils]))


def _window_bias():
    qi = jnp.arange(2 * WIN)[:, None] % WIN
    kj = jnp.arange(2 * WIN)[None, :]
    cur = (kj >= WIN) & (kj - WIN <= qi)
    prev = (kj < WIN) & (kj >= qi)
    return jnp.stack([jnp.where(cur, 0.0, NEG), jnp.where(cur | prev, 0.0, NEG)]).astype(f32)


def kernel(x, norm_pre, norm_post, dil_w_in, dil_w_out, diff_w_in, diff_w_out, diff_lambda_q1, diff_lambda_k1, diff_lambda_q2, diff_lambda_k2, diff_subln):
    batch, seq, d_model = x.shape
    assert d_model == D_MODEL and seq % ROW_TILE == 0
    n_g = len(DILATIONS)
    h = x.reshape(batch * seq, d_model)

    names = [kind + str(g) for kind in "qkv" for g in range(n_g)] + ["gate"]
    order = sorted(range(len(names)), key=lambda s: (0 if names[s] == "gate" else int(names[s][1]), s))
    sec_of = {names[s]: pos for pos, s in enumerate(order)}
    sec_dsel = [0 if names[s] == "gate" else int(names[s][1]) for s in order]
    rope_secs = [pos for pos, s in enumerate(order) if names[s][0] in "qk"]
    w_in = dil_w_in[0].reshape(d_model, len(names), D_MODEL)[:, jnp.array(order)].reshape(d_model, -1)
    cos_t, sin_t = _rope_tables(seq, DILATIONS)
    u = _prenorm(h, norm_pre[0][None, :], DILATIONS)
    proj = _in_proj(u, w_in.astype(bf16), cos_t, sin_t, batch, seq, sec_dsel, rope_secs)
    y = _dil_attention(proj, _window_bias(), batch, seq, sec_of)
    h = _out_proj(y.reshape(batch * seq, d_model), dil_w_out[0].astype(bf16), norm_post[0][None, :], h)

    lambda_init = 0.8 - 0.6 * math.exp(-0.3 * 1)
    tq, tk = 256, 512
    u = _prenorm(h, norm_pre[1][None, :], (1,))
    proj = _in_proj(u, diff_w_in[0].astype(bf16), cos_t[:1], sin_t[:1], batch, seq, [0, 0, 0, 0], [0, 1])
    y = _diff_attention(proj, diff_lambda_q1[0][None, :], diff_lambda_k1[0][None, :],
                        diff_lambda_q2[0][None, :], diff_lambda_k2[0][None, :],
                        diff_subln[0][None, :], batch, seq, lambda_init, tq, tk)
    h = _out_proj(y.reshape(batch * seq, d_model), diff_w_out[0].astype(bf16), norm_post[1][None, :], h)
    return h.reshape(batch, seq, d_model)
```

```python
import functools
import math

import jax
import jax.numpy as jnp
from jax import lax
from jax.experimental import pallas as pl
from jax.experimental.pallas import tpu as pltpu

D_MODEL = 1024
HEAD_DIM = 64
ROPE_THETA = 10000.0
EPS = 1e-6
DILATIONS = (1, 4, 16)
WIN = 128
LANES = 128
N_CHUNK = D_MODEL // LANES
ROW_TILE = 2048
BLK_PER_TILE = ROW_TILE // WIN
NEG = -1e30
VMEM_LIMIT = 52 * 1024 * 1024

f32 = jnp.float32
bf16 = jnp.bfloat16


def _cparams(n_axes):
    return pltpu.CompilerParams(
        dimension_semantics=("arbitrary",) * n_axes, vmem_limit_bytes=VMEM_LIMIT)


def _prenorm_kernel(*refs, dils):
    x_refs, (g_ref, o_ref, rs_sc) = refs[:N_CHUNK], refs[N_CHUNK:]
    k = pl.program_id(1)
    d_model = N_CHUNK * LANES
    sub = 2 * WIN

    @pl.when(k == 0)
    def _():
        def scale(i, carry):
            rows = pl.ds(pl.multiple_of(i * sub, sub), sub)
            ss = jnp.zeros((sub, 1), f32)
            for x_ref in x_refs:
                x = x_ref[rows, :]
                ss = ss + jnp.sum(x * x, axis=-1, keepdims=True)
            rs_sc[rows, :] = jnp.broadcast_to(lax.rsqrt(ss * (1.0 / d_model) + EPS), (sub, LANES))
            return carry

        lax.fori_loop(0, ROW_TILE // sub, scale, 0)

    def emit(d):
        per_res = BLK_PER_TILE // d

        def body(s, carry):
            src = (s % per_res) * WIN * d + s // per_res
            rows = pl.ds(src, WIN, stride=d) if d > 1 else pl.ds(pl.multiple_of(src, WIN), WIN)
            rs = rs_sc[rows, :]
            o0 = pl.multiple_of(s * WIN, WIN)
            for c, x_ref in enumerate(x_refs):
                cols = slice(c * LANES, (c + 1) * LANES)
                o_ref[pl.ds(o0, WIN), cols] = ((x_ref[rows, :] * rs) * g_ref[:, cols]).astype(bf16)
            return carry

        lax.fori_loop(0, BLK_PER_TILE, body, 0)

    for kk, d in enumerate(dils):
        pl.when(k == kk)(functools.partial(emit, d))


def _prenorm(x2d, gain, dils):
    m, d_model = x2d.shape
    chunk = lambda c: pl.BlockSpec((ROW_TILE, LANES), lambda t, k: (t, c))
    return pl.pallas_call(
        functools.partial(_prenorm_kernel, dils=dils),
        grid=(m // ROW_TILE, len(dils)),
        in_specs=[chunk(c) for c in range(N_CHUNK)] + [pl.BlockSpec((1, d_model), lambda t, k: (0, 0))],
        out_specs=pl.BlockSpec((None, ROW_TILE, d_model), lambda t, k: (k, t, 0)),
        out_shape=jax.ShapeDtypeStruct((len(dils), m, d_model), bf16),
        scratch_shapes=[pltpu.VMEM((ROW_TILE, LANES), f32)],
        compiler_params=_cparams(2),
        name="prenorm",
    )(*([x2d] * N_CHUNK), gain)


def _proj_kernel(u_ref, w_ref, cos_ref, sin_ref, o_ref, *, rope_secs):
    j = pl.program_id(1)
    n_c = o_ref.shape[0]
    lane = lax.broadcasted_iota(jnp.int32, (1, LANES), 1)
    first_half = (lane % HEAD_DIM) < (HEAD_DIM // 2)
    is_rope = functools.reduce(jnp.logical_or, [j == s for s in rope_secs])

    @pl.when(is_rope)
    def _():
        cos, sin = cos_ref[...], sin_ref[...]
        half = n_c // 2
        for part in range(2):
            cols = slice(part * half * LANES, (part + 1) * half * LANES)
            acc = jnp.dot(u_ref[...], w_ref[:, cols], preferred_element_type=f32)
            for c in range(half):
                x = acc[:, c * LANES:(c + 1) * LANES]
                xb = x.astype(bf16)
                partner = jnp.where(first_half,
                                    pltpu.roll(xb, LANES - HEAD_DIM // 2, axis=1),
                                    pltpu.roll(xb, HEAD_DIM // 2, axis=1)).astype(f32)
                o_ref[part * half + c] = (x * cos + partner * sin).astype(bf16)

    @pl.when(jnp.logical_not(is_rope))
    def _():
        acc = jnp.dot(u_ref[...], w_ref[...], preferred_element_type=f32)
        for c in range(n_c):
            o_ref[c] = acc[:, c * LANES:(c + 1) * LANES].astype(bf16)


def _lookup(j, table):
    return sum([jnp.where(j == pos, val, 0) for pos, val in enumerate(table)], jnp.int32(0))


def _in_proj(u, w_bf16, cos_t, sin_t, batch, seq, visit, sec_dsel, rope_steps, bm=ROW_TILE):
    _, m, d_in = u.shape
    n_out = w_bf16.shape[1]
    bn = D_MODEL
    tiles_per_seq = seq // bm

    def sec(j):
        return _lookup(j, visit)

    def dsel(j):
        return _lookup(j, sec_dsel)

    kern = functools.partial(_proj_kernel, rope_secs=rope_steps)
    return pl.pallas_call(
        kern,
        grid=(m // bm, n_out // bn),
        in_specs=[
            pl.BlockSpec((None, bm, d_in), lambda i, j: (dsel(j), i, 0)),
            pl.BlockSpec((d_in, bn), lambda i, j: (0, sec(j))),
            pl.BlockSpec((None, bm, LANES), lambda i, j: (dsel(j), i % tiles_per_seq, 0)),
            pl.BlockSpec((None, bm, LANES), lambda i, j: (dsel(j), i % tiles_per_seq, 0)),
        ],
        out_specs=pl.BlockSpec((None, bn // LANES, bm, LANES),
                               lambda i, j: (i // tiles_per_seq, sec(j), i % tiles_per_seq, 0)),
        out_shape=jax.ShapeDtypeStruct((batch, n_out // LANES, seq, LANES), bf16),
        compiler_params=_cparams(2),
        name="in_proj",
    )(u, w_bf16, cos_t, sin_t)


def _dil_attn_kernel(bias_ref, q0, k0, v0, q1, k1, v1, q2, k2, v2, gate_ref, y_ref,
                     num_sc, den_sc, max_sc, *, unroll):
    seq = y_ref.shape[0]
    n_blk = seq // WIN
    lane = lax.broadcasted_iota(jnp.int32, (1, LANES), 1)
    low = lane < HEAD_DIM
    scale = HEAD_DIM ** -0.5
    sel0 = jnp.where(low, scale, 0.0).astype(bf16)
    sel1 = jnp.where(low, 0.0, scale).astype(bf16)
    ones = jnp.ones((2 * WIN, LANES), bf16)
    groups = ((q0, k0, v0), (q1, k1, v1), (q2, k2, v2))

    def block(blk, carry):
        t = blk // BLK_PER_TILE
        u = blk % BLK_PER_TILE
        row0 = pl.multiple_of(blk * WIN, WIN)
        for g, d in enumerate(DILATIONS):
            q_ref, k_ref, v_ref = groups[g]
            per_res = BLK_PER_TILE // d
            r = u // per_res
            n = u % per_res
            prev = jnp.where(n > 0, blk - 1, blk - BLK_PER_TILE + per_res - 1)
            has_prev = (n > 0) | (t > 0)
            prow0 = pl.multiple_of(jnp.maximum(prev, 0) * WIN, WIN)
            nat0 = t * ROW_TILE + n * WIN * d + r

            q = q_ref[pl.ds(row0, WIN), :]
            qs = jnp.concatenate([q * sel0, q * sel1], axis=0)
            kcat = jnp.concatenate([k_ref[pl.ds(prow0, WIN), :], k_ref[pl.ds(row0, WIN), :]], axis=0)
            vcat = jnp.concatenate([v_ref[pl.ds(prow0, WIN), :], v_ref[pl.ds(row0, WIN), :]], axis=0)
            s = lax.dot_general(qs, kcat, (((1,), (1,)), ((), ())), preferred_element_type=f32)
            s = s + bias_ref[jnp.where(has_prev, 1, 0)]
            m = jnp.max(s, axis=-1, keepdims=True)
            p = jnp.exp(s - m).astype(bf16)
            vext = jnp.concatenate([vcat, ones], axis=1)
            pv = jnp.dot(p, vext, preferred_element_type=f32)
            rows = pl.ds(nat0, WIN, stride=d) if d > 1 else pl.ds(nat0, WIN)
            num_sc[g, rows, :] = jnp.where(low, pv[:WIN, :LANES], pv[WIN:, :LANES])
            den_sc[g, rows, :] = jnp.where(low, pv[:WIN, LANES:], pv[WIN:, LANES:])
            max_sc[g, rows, :] = jnp.where(low, m[:WIN], m[WIN:])
        return carry

    lax.fori_loop(0, n_blk, block, 0, unroll=unroll)

    def merge(i, carry):
        rows = pl.ds(pl.multiple_of(i * WIN, WIN), WIN)
        m0, m1, m2 = max_sc[0, rows, :], max_sc[1, rows, :], max_sc[2, rows, :]
        mx = jnp.maximum(jnp.maximum(m0, m1), m2)
        e0, e1, e2 = jnp.exp(m0 - mx), jnp.exp(m1 - mx), jnp.exp(m2 - mx)
        num = e0 * num_sc[0, rows, :] + e1 * num_sc[1, rows, :] + e2 * num_sc[2, rows, :]
        den = e0 * den_sc[0, rows, :] + e1 * den_sc[1, rows, :] + e2 * den_sc[2, rows, :]
        gate = gate_ref[rows, :].astype(f32)
        y_ref[rows, :] = ((num / den) * (gate * jax.nn.sigmoid(gate))).astype(bf16)
        return carry

    lax.fori_loop(0, n_blk, merge, 0, unroll=2)


def _dil_attention(proj, bias, batch, seq, sec_of):
    def slab(sec):
        return pl.BlockSpec((None, None, seq, LANES), lambda b, hp: (b, sec * N_CHUNK + hp, 0, 0))

    specs = [pl.BlockSpec((2, 2 * WIN, 2 * WIN), lambda b, hp: (0, 0, 0))]
    args = [bias]
    for g in range(len(DILATIONS)):
        for kind in "qkv":
            specs.append(slab(sec_of[kind + str(g)]))
            args.append(proj)
    specs.append(slab(sec_of["gate"]))
    args.append(proj)
    return pl.pallas_call(
        functools.partial(_dil_attn_kernel, unroll=16),
        grid=(batch, N_CHUNK),
        in_specs=specs,
        out_specs=pl.BlockSpec((None, seq, LANES), lambda b, hp: (b, 0, hp)),
        out_shape=jax.ShapeDtypeStruct((batch, seq, D_MODEL), bf16),
        scratch_shapes=[pltpu.VMEM((len(DILATIONS), seq, LANES), f32)] * 3,
        compiler_params=_cparams(2),
        name="dilated_attention",
    )(*args)


TAB_QI, TAB_KJ = range(2)


def _full_tiles(seq, tq, tk):
    pairs = [(qi, kj) for qi in range(seq // tq) for kj in range((qi * tq) // tk)]
    return list(zip(*pairs))


def _diff_attn_kernel(tab_ref, lq1_ref, lk1_ref, lq2_ref, lk2_ref, subln_ref, bias_ref,
                      q_ref, k_ref, v_ref, gate_ref, y_ref, acc_sc, m_sc, *,
                      tq, tk, n_full, unroll, lambda_init):
    lane = lax.broadcasted_iota(jnp.int32, (1, LANES), 1)
    low = lane < HEAD_DIM
    scale = HEAD_DIM ** -0.5
    sel0 = jnp.where(low, scale, 0.0).astype(bf16)
    sel1 = jnp.where(low, 0.0, scale).astype(bf16)
    ones = jnp.ones((tk, LANES), bf16)
    lam = (jnp.exp(jnp.sum(lq1_ref[...] * lk1_ref[...], axis=-1, keepdims=True))
           - jnp.exp(jnp.sum(lq2_ref[...] * lk2_ref[...], axis=-1, keepdims=True)) + lambda_init)

    def scores(q0, k0, width):
        q = q_ref[pl.ds(q0, tq), :]
        qs = jnp.concatenate([q * sel0, q * sel1], axis=0)
        return lax.dot_general(qs, k_ref[pl.ds(k0, width), :], (((1,), (1,)), ((), ())),
                               preferred_element_type=f32)

    def weighted_values(p, k0, width):
        vext = jnp.concatenate([v_ref[pl.ds(k0, width), :], ones[:width]], axis=1)
        return jnp.dot(p, vext, preferred_element_type=f32)

    for qi in range(y_ref.shape[0] // tq):
        k0 = (qi * tq) // tk * tk
        width = (qi + 1) * tq - k0
        bias = bias_ref[(qi * tq - k0) // tq, :, :width]
        s = scores(qi * tq, k0, width) + jnp.tile(bias, (2, 1))
        m = jnp.max(s, axis=-1, keepdims=True)
        m_sc[qi] = jnp.broadcast_to(m, m_sc.shape[1:])
        acc_sc[qi] = weighted_values(jnp.exp(s - m).astype(bf16), k0, width)

    def step(t):
        qi = tab_ref[TAB_QI, t]
        q0 = pl.multiple_of(qi * tq, tq)
        k0 = pl.multiple_of(tab_ref[TAB_KJ, t] * tk, tk)
        s = scores(q0, k0, tk)
        m_old = m_sc[qi]
        m_new = jnp.maximum(m_old, jnp.max(s, axis=-1, keepdims=True))
        alpha = jnp.exp(m_old - m_new)
        m_sc[qi] = m_new
        p = jnp.exp(s - jnp.tile(m_new, (1, tk // LANES))).astype(bf16)
        acc_sc[qi] = jnp.tile(alpha, (1, 2)) * acc_sc[qi] + weighted_values(p, k0, tk)

    def steps(i, carry):
        for k in range(unroll):
            step(unroll * i + k)
        return carry

    assert n_full % unroll == 0
    lax.fori_loop(0, n_full // unroll, steps, 0)

    def finalize(qi, carry):
        q0 = pl.multiple_of(qi * tq, tq)
        acc = acc_sc[qi]
        o = acc[:, :LANES] / acc[:, LANES:]
        o = o[:tq] - lam * o[tq:]
        o = o * lax.rsqrt(jnp.mean(o * o, axis=-1, keepdims=True) + EPS)
        o = o * subln_ref[...] * (1.0 - lambda_init)
        gate = gate_ref[pl.ds(q0, tq), :].astype(f32)
        y_ref[pl.ds(q0, tq), :] = (o * (gate * jax.nn.sigmoid(gate))).astype(bf16)
        return carry

    lax.fori_loop(0, y_ref.shape[0] // tq, finalize, 0)


def _diff_attention(proj, lq1, lk1, lq2, lk2, subln, batch, seq, lambda_init, tq, tk):
    def slab(c0):
        return pl.BlockSpec((None, None, seq, LANES), lambda b, h: (b, c0 + h, 0, 0))

    def small(a):
        return pl.BlockSpec(a.shape, lambda b, h: (0,) * a.ndim)

    table = jnp.array(_full_tiles(seq, tq, tk), jnp.int32)
    n_full = table.shape[1]
    qi = jnp.arange(tq)[:, None]
    kj = jnp.arange(tk)[None, :]
    bias = jnp.stack([jnp.where(kj <= qi + o * tq, 0.0, NEG).astype(f32) for o in range(tk // tq)])
    smalls = [lq1, lk1, lq2, lk2, subln, bias]
    kern = functools.partial(_diff_attn_kernel, tq=tq, tk=tk, n_full=n_full, unroll=14,
                             lambda_init=lambda_init)
    return pl.pallas_call(
        kern,
        grid=(batch, N_CHUNK),
        in_specs=([pl.BlockSpec(memory_space=pltpu.SMEM)] + [small(a) for a in smalls]
                  + [slab(0), slab(N_CHUNK), slab(2 * N_CHUNK), slab(3 * N_CHUNK)]),
        out_specs=pl.BlockSpec((None, seq, LANES), lambda b, h: (b, 0, h)),
        out_shape=jax.ShapeDtypeStruct((batch, seq, D_MODEL), bf16),
        scratch_shapes=[pltpu.VMEM((seq // tq, 2 * tq, 2 * LANES), f32),
                        pltpu.VMEM((seq // tq, 2 * tq, LANES), f32)],
        compiler_params=_cparams(2),
        name="diff_attention",
    )(table, *smalls, proj, proj, proj, proj)


def _residual_update(y_ref, w_ref, g_ref, h_ref):
    z = jnp.dot(y_ref[...], w_ref[...], preferred_element_type=f32)
    z = z * lax.rsqrt(jnp.mean(z * z, axis=-1, keepdims=True) + EPS)
    return h_ref[...] + z * g_ref[...]


def _out_proj_kernel(y_ref, w_ref, g_ref, h_ref, o_ref):
    o_ref[...] = _residual_update(y_ref, w_ref, g_ref, h_ref)


def _out_proj_norm_kernel(y_ref, w_ref, g_ref, h_ref, g_next_ref, o_ref, u_ref):
    h = _residual_update(y_ref, w_ref, g_ref, h_ref)
    o_ref[...] = h
    u_ref[...] = ((h * lax.rsqrt(jnp.mean(h * h, axis=-1, keepdims=True) + EPS)) * g_next_ref[...]).astype(bf16)


def _out_proj(y2d, w_bf16, gain, h2d, next_gain=None, bm=1024):
    m, d = h2d.shape
    row = pl.BlockSpec((bm, d), lambda i: (i, 0))
    vec = pl.BlockSpec((1, d), lambda i: (0, 0))
    in_specs = [row, pl.BlockSpec((d, d), lambda i: (0, 0)), vec, row]
    if next_gain is None:
        return pl.pallas_call(
            _out_proj_kernel, grid=(m // bm,), in_specs=in_specs, out_specs=row,
            out_shape=jax.ShapeDtypeStruct((m, d), f32),
            compiler_params=_cparams(1), name="out_proj",
        )(y2d, w_bf16, gain, h2d)
    return pl.pallas_call(
        _out_proj_norm_kernel, grid=(m // bm,), in_specs=in_specs + [vec],
        out_specs=[row, pl.BlockSpec((None, bm, d), lambda i: (0, i, 0))],
        out_shape=[jax.ShapeDtypeStruct((m, d), f32), jax.ShapeDtypeStruct((1, m, d), bf16)],
        compiler_params=_cparams(1), name="out_proj_norm",
    )(y2d, w_bf16, gain, h2d, next_gain)


def _rope_tables(seq, dils):
    half = HEAD_DIM // 2
    freqs = ROPE_THETA ** (-jnp.arange(0, HEAD_DIM, 2, dtype=f32) / HEAD_DIM)
    lane = jnp.arange(LANES)
    freq_l = freqs[lane % half][None, None, :]
    sign_l = jnp.where(lane % HEAD_DIM < half, -1.0, 1.0).astype(f32)[None, None, :]
    row = jnp.arange(seq)[None, :]
    d = jnp.array(dils)[:, None]
    w = row % ROW_TILE
    pos = row - w + (w % (ROW_TILE // d)) * d + w // (ROW_TILE // d)
    ang = pos.astype(f32)[:, :, None] * freq_l
    return jnp.cos(ang), sign_l * jnp.sin(ang)


def _window_bias():
    qi = jnp.arange(2 * WIN)[:, None] % WIN
    kj = jnp.arange(2 * WIN)[None, :]
    cur = (kj >= WIN) & (kj - WIN <= qi)
    prev = (kj < WIN) & (kj >= qi)
    return jnp.stack([jnp.where(cur, 0.0, NEG), jnp.where(cur | prev, 0.0, NEG)]).astype(f32)


def kernel(x, norm_pre, norm_post, dil_w_in, dil_w_out, diff_w_in, diff_w_out, diff_lambda_q1, diff_lambda_k1, diff_lambda_q2, diff_lambda_k2, diff_subln):
    batch, seq, d_model = x.shape
    assert d_model == D_MODEL and seq % ROW_TILE == 0
    n_g = len(DILATIONS)
    h = x.reshape(batch * seq, d_model)

    names = [kind + str(g) for kind in "qkv" for g in range(n_g)] + ["gate"]
    group = [0 if name == "gate" else int(name[1]) for name in names]
    visit = sorted(range(len(names)), key=lambda s: (group[s], s))
    sec_of = {name: s for s, name in enumerate(names)}
    sec_dsel = [group[s] for s in visit]
    rope_steps = [j for j, s in enumerate(visit) if names[s][0] in "qk"]
    cos_t, sin_t = _rope_tables(seq, DILATIONS)
    u = _prenorm(h, norm_pre[0][None, :], DILATIONS)
    proj = _in_proj(u, dil_w_in[0].astype(bf16), cos_t, sin_t, batch, seq, visit, sec_dsel, rope_steps)
    y = _dil_attention(proj, _window_bias(), batch, seq, sec_of)
    h, u = _out_proj(y.reshape(batch * seq, d_model), dil_w_out[0].astype(bf16), norm_post[0][None, :], h,
                     next_gain=norm_pre[1][None, :])

    lambda_init = 0.8 - 0.6 * math.exp(-0.3 * 1)
    tq, tk = 256, 512
    proj = _in_proj(u, diff_w_in[0].astype(bf16), cos_t[:1], sin_t[:1], batch, seq,
                    [0, 1, 2, 3], [0, 0, 0, 0], [0, 1])
    y = _diff_attention(proj, diff_lambda_q1[0][None, :], diff_lambda_k1[0][None, :],
                        diff_lambda_q2[0][None, :], diff_lambda_k2[0][None, :],
                        diff_subln[0][None, :], batch, seq, lambda_init, tq, tk)
    h = _out_proj(y.reshape(batch * seq, d_model), diff_w_out[0].astype(bf16), norm_post[1][None, :], h)
    return h.reshape(batch, seq, d_model)
```

```python
import functools
import math

import jax
import jax.numpy as jnp
from jax import lax
from jax.experimental import pallas as pl
from jax.experimental.pallas import tpu as pltpu

D_MODEL = 1024
HEAD_DIM = 64
ROPE_THETA = 10000.0
EPS = 1e-6
DILATIONS = (1, 4, 16)
WIN = 128
LANES = 128
N_CHUNK = D_MODEL // LANES
ROW_TILE = 2048
BLK_PER_TILE = ROW_TILE // WIN
NEG = -1e30
VMEM_LIMIT = 52 * 1024 * 1024

f32 = jnp.float32
bf16 = jnp.bfloat16


def _cparams(n_axes):
    return pltpu.CompilerParams(
        dimension_semantics=("arbitrary",) * n_axes, vmem_limit_bytes=VMEM_LIMIT)


def _prenorm_kernel(*refs, dils):
    x_refs, (g_ref, o_ref, rs_sc, xp_sc, rsp_sc) = refs[:N_CHUNK], refs[N_CHUNK:]
    k = pl.program_id(1)
    d_model = N_CHUNK * LANES
    sub = 2 * WIN

    @pl.when(k == 0)
    def _():
        def scale(i, carry):
            rows = pl.ds(pl.multiple_of(i * sub, sub), sub)
            sq = jnp.zeros((sub, LANES), f32)
            for x_ref in x_refs:
                x = x_ref[rows, :]
                sq = sq + x * x
            ss = jnp.sum(sq, axis=-1, keepdims=True)
            rs_sc[rows, :] = jnp.broadcast_to(lax.rsqrt(ss * (1.0 / d_model) + EPS), (sub, LANES))
            return carry

        lax.fori_loop(0, ROW_TILE // sub, scale, 0)

    def emit(kk):
        d = dils[kk]
        d_prev = dils[kk - 1] if kk else 1
        q = d // d_prev
        from_copy = d_prev > 1
        keep_copy = kk + 1 < len(dils) and d > 1
        per_res = BLK_PER_TILE // d

        def body(s, carry):
            r, n = s // per_res, s % per_res
            src = (r % d_prev) * (ROW_TILE // d_prev) + r // d_prev + q * n * WIN
            rows = pl.ds(src, WIN, stride=q) if q > 1 else pl.ds(pl.multiple_of(src, WIN), WIN)
            dst = pl.ds(pl.multiple_of(s * WIN, WIN), WIN)
            rs = (rsp_sc if from_copy else rs_sc)[rows, :]
            if keep_copy:
                rsp_sc[dst, :] = rs
            for c, x_ref in enumerate(x_refs):
                x = xp_sc[c, rows, :] if from_copy else x_ref[rows, :]
                if keep_copy:
                    xp_sc[c, dst, :] = x
                cols = slice(c * LANES, (c + 1) * LANES)
                o_ref[dst, cols] = ((x * rs) * g_ref[:, cols]).astype(bf16)
            return carry

        lax.fori_loop(0, BLK_PER_TILE, body, 0)

    assert all(d % p == 0 for p, d in zip(dils, dils[1:])) and sum(d > 1 for d in dils[:-1]) <= 1
    for kk in range(len(dils)):
        pl.when(k == kk)(functools.partial(emit, kk))


def _prenorm(x2d, gain, dils):
    m, d_model = x2d.shape
    chunk = lambda c: pl.BlockSpec((ROW_TILE, LANES), lambda t, k: (t, c))
    return pl.pallas_call(
        functools.partial(_prenorm_kernel, dils=dils),
        grid=(m // ROW_TILE, len(dils)),
        in_specs=[chunk(c) for c in range(N_CHUNK)] + [pl.BlockSpec((1, d_model), lambda t, k: (0, 0))],
        out_specs=pl.BlockSpec((None, ROW_TILE, d_model), lambda t, k: (k, t, 0)),
        out_shape=jax.ShapeDtypeStruct((len(dils), m, d_model), bf16),
        scratch_shapes=[pltpu.VMEM((ROW_TILE, LANES), f32),
                        pltpu.VMEM((N_CHUNK, ROW_TILE, LANES), f32),
                        pltpu.VMEM((ROW_TILE, LANES), f32)],
        compiler_params=_cparams(2),
        name="prenorm",
    )(*([x2d] * N_CHUNK), gain)


def _proj_kernel(u_ref, w_ref, cos_ref, sin_ref, o_ref, *, rope_secs):
    j = pl.program_id(1)
    n_c = o_ref.shape[0]
    lane = lax.broadcasted_iota(jnp.int32, (1, LANES), 1)
    first_half = (lane % HEAD_DIM) < (HEAD_DIM // 2)
    is_rope = functools.reduce(jnp.logical_or, [j == s for s in rope_secs])

    @pl.when(is_rope)
    def _():
        cos, sin = cos_ref[...], sin_ref[...]
        half = n_c // 2
        for part in range(2):
            cols = slice(part * half * LANES, (part + 1) * half * LANES)
            acc = jnp.dot(u_ref[...], w_ref[:, cols], preferred_element_type=f32)
            for c in range(half):
                x = acc[:, c * LANES:(c + 1) * LANES].astype(bf16)
                partner = jnp.where(first_half,
                                    pltpu.roll(x, LANES - HEAD_DIM // 2, axis=1),
                                    pltpu.roll(x, HEAD_DIM // 2, axis=1))
                o_ref[part * half + c] = x * cos + partner * sin

    @pl.when(jnp.logical_not(is_rope))
    def _():
        acc = jnp.dot(u_ref[...], w_ref[...], preferred_element_type=f32)
        for c in range(n_c):
            o_ref[c] = acc[:, c * LANES:(c + 1) * LANES].astype(bf16)


def _lookup(j, table):
    return sum([jnp.where(j == pos, val, 0) for pos, val in enumerate(table)], jnp.int32(0))


def _in_proj(u, w_bf16, cos_t, sin_t, batch, seq, visit, sec_dsel, rope_steps, bm=ROW_TILE):
    _, m, d_in = u.shape
    n_out = w_bf16.shape[1]
    bn = D_MODEL
    tiles_per_seq = seq // bm

    def sec(j):
        return _lookup(j, visit)

    def dsel(j):
        return _lookup(j, sec_dsel)

    kern = functools.partial(_proj_kernel, rope_secs=rope_steps)
    return pl.pallas_call(
        kern,
        grid=(m // bm, n_out // bn),
        in_specs=[
            pl.BlockSpec((None, bm, d_in), lambda i, j: (dsel(j), i, 0)),
            pl.BlockSpec((d_in, bn), lambda i, j: (0, sec(j))),
            pl.BlockSpec((None, bm, LANES), lambda i, j: (dsel(j), i % tiles_per_seq, 0)),
            pl.BlockSpec((None, bm, LANES), lambda i, j: (dsel(j), i % tiles_per_seq, 0)),
        ],
        out_specs=pl.BlockSpec((None, bn // LANES, bm, LANES),
                               lambda i, j: (i // tiles_per_seq, sec(j), i % tiles_per_seq, 0)),
        out_shape=jax.ShapeDtypeStruct((batch, n_out // LANES, seq, LANES), bf16),
        compiler_params=_cparams(2),
        name="in_proj",
    )(u, w_bf16, cos_t, sin_t)


def _dil_attn_kernel(bias_ref, q0, k0, v0, q1, k1, v1, q2, k2, v2, gate_ref, y_ref,
                     num_sc, den_sc, max_sc, *, unroll):
    seq = y_ref.shape[0]
    n_blk = seq // WIN
    lane = lax.broadcasted_iota(jnp.int32, (1, LANES), 1)
    low = lane < HEAD_DIM
    scale = HEAD_DIM ** -0.5
    sel0 = jnp.where(low, scale, 0.0).astype(bf16)
    sel1 = jnp.where(low, 0.0, scale).astype(bf16)
    ones = jnp.ones((2 * WIN, LANES), bf16)
    groups = ((q0, k0, v0), (q1, k1, v1), (q2, k2, v2))

    def block(blk, carry):
        t = blk // BLK_PER_TILE
        u = blk % BLK_PER_TILE
        row0 = pl.multiple_of(blk * WIN, WIN)
        for g, d in enumerate(DILATIONS):
            q_ref, k_ref, v_ref = groups[g]
            per_res = BLK_PER_TILE // d
            r = u // per_res
            n = u % per_res
            prev = jnp.where(n > 0, blk - 1, blk - BLK_PER_TILE + per_res - 1)
            has_prev = (n > 0) | (t > 0)
            prow0 = pl.multiple_of(jnp.maximum(prev, 0) * WIN, WIN)
            nat0 = t * ROW_TILE + n * WIN * d + r

            q = q_ref[pl.ds(row0, WIN), :]
            qs = jnp.concatenate([q * sel0, q * sel1], axis=0)
            kcat = jnp.concatenate([k_ref[pl.ds(prow0, WIN), :], k_ref[pl.ds(row0, WIN), :]], axis=0)
            vcat = jnp.concatenate([v_ref[pl.ds(prow0, WIN), :], v_ref[pl.ds(row0, WIN), :]], axis=0)
            s = lax.dot_general(qs, kcat, (((1,), (1,)), ((), ())), preferred_element_type=f32)
            s = s + bias_ref[jnp.where(has_prev, 1, 0)]
            m = jnp.max(s, axis=-1, keepdims=True)
            p = jnp.exp(s - m).astype(bf16)
            vext = jnp.concatenate([vcat, ones], axis=1)
            pv = jnp.dot(p, vext, preferred_element_type=f32)
            rows = pl.ds(nat0, WIN, stride=d) if d > 1 else pl.ds(nat0, WIN)
            num_sc[g, rows, :] = jnp.where(low, pv[:WIN, :LANES], pv[WIN:, :LANES])
            den_sc[g, rows, :] = jnp.where(low, pv[:WIN, LANES:], pv[WIN:, LANES:])
            max_sc[g, rows, :] = jnp.where(low, m[:WIN], m[WIN:])
        return carry

    lax.fori_loop(0, n_blk, block, 0, unroll=unroll)

    def merge(i, carry):
        rows = pl.ds(pl.multiple_of(i * WIN, WIN), WIN)
        m0, m1, m2 = max_sc[0, rows, :], max_sc[1, rows, :], max_sc[2, rows, :]
        mx = jnp.maximum(jnp.maximum(m0, m1), m2)
        e0, e1, e2 = jnp.exp(m0 - mx), jnp.exp(m1 - mx), jnp.exp(m2 - mx)
        num = e0 * num_sc[0, rows, :] + e1 * num_sc[1, rows, :] + e2 * num_sc[2, rows, :]
        den = e0 * den_sc[0, rows, :] + e1 * den_sc[1, rows, :] + e2 * den_sc[2, rows, :]
        gate = gate_ref[rows, :].astype(f32)
        y_ref[rows, :] = ((num * gate) / (den * (1.0 + jnp.exp(-gate)))).astype(bf16)
        return carry

    lax.fori_loop(0, n_blk, merge, 0, unroll=2)


def _dil_attention(proj, bias, batch, seq, sec_of):
    def slab(sec):
        return pl.BlockSpec((None, None, seq, LANES), lambda b, hp: (b, sec * N_CHUNK + hp, 0, 0))

    specs = [pl.BlockSpec((2, 2 * WIN, 2 * WIN), lambda b, hp: (0, 0, 0))]
    args = [bias]
    for g in range(len(DILATIONS)):
        for kind in "qkv":
            specs.append(slab(sec_of[kind + str(g)]))
            args.append(proj)
    specs.append(slab(sec_of["gate"]))
    args.append(proj)
    return pl.pallas_call(
        functools.partial(_dil_attn_kernel, unroll=16),
        grid=(batch, N_CHUNK),
        in_specs=specs,
        out_specs=pl.BlockSpec((None, seq, LANES), lambda b, hp: (b, 0, hp)),
        out_shape=jax.ShapeDtypeStruct((batch, seq, D_MODEL), bf16),
        scratch_shapes=[pltpu.VMEM((len(DILATIONS), seq, LANES), f32)] * 3,
        compiler_params=_cparams(2),
        name="dilated_attention",
    )(*args)


TAB_QI, TAB_KJ = range(2)


def _full_tiles(seq, tq, tk):
    pairs = [(qi, kj) for qi in range(seq // tq) for kj in range((qi * tq) // tk)]
    return list(zip(*pairs))


def _diff_attn_kernel(tab_ref, lq1_ref, lk1_ref, lq2_ref, lk2_ref, subln_ref, bias_ref,
                      q_ref, k_ref, v_ref, gate_ref, y_ref, acc_sc, m_sc, *,
                      tq, tk, n_full, unroll, lambda_init):
    lane = lax.broadcasted_iota(jnp.int32, (1, LANES), 1)
    low = lane < HEAD_DIM
    scale = HEAD_DIM ** -0.5
    sel0 = jnp.where(low, scale, 0.0).astype(bf16)
    sel1 = jnp.where(low, 0.0, scale).astype(bf16)
    ones = jnp.ones((tk, LANES), bf16)
    lam = (jnp.exp(jnp.sum(lq1_ref[...] * lk1_ref[...], axis=-1, keepdims=True))
           - jnp.exp(jnp.sum(lq2_ref[...] * lk2_ref[...], axis=-1, keepdims=True)) + lambda_init)

    def scores(q0, k0, width):
        q = q_ref[pl.ds(q0, tq), :]
        qs = jnp.concatenate([q * sel0, q * sel1], axis=0)
        return lax.dot_general(qs, k_ref[pl.ds(k0, width), :], (((1,), (1,)), ((), ())),
                               preferred_element_type=f32)

    def weighted_values(p, k0, width):
        vext = jnp.concatenate([v_ref[pl.ds(k0, width), :], ones[:width]], axis=1)
        return jnp.dot(p, vext, preferred_element_type=f32)

    for qi in range(y_ref.shape[0] // tq):
        k0 = (qi * tq) // tk * tk
        width = (qi + 1) * tq - k0
        bias = bias_ref[(qi * tq - k0) // tq, :, :width]
        s = scores(qi * tq, k0, width) + jnp.tile(bias, (2, 1))
        m = jnp.max(s, axis=-1, keepdims=True)
        m_sc[qi] = jnp.broadcast_to(m, m_sc.shape[1:])
        acc_sc[qi] = weighted_values(jnp.exp(s - m).astype(bf16), k0, width)

    def step(t):
        qi = tab_ref[TAB_QI, t]
        q0 = pl.multiple_of(qi * tq, tq)
        k0 = pl.multiple_of(tab_ref[TAB_KJ, t] * tk, tk)
        s = scores(q0, k0, tk)
        m_old = m_sc[qi]
        m_new = jnp.maximum(m_old, jnp.max(s, axis=-1, keepdims=True))
        alpha = jnp.exp(m_old - m_new)
        m_sc[qi] = m_new
        p = jnp.exp(s - jnp.tile(m_new, (1, tk // LANES))).astype(bf16)
        acc_sc[qi] = jnp.tile(alpha, (1, 2)) * acc_sc[qi] + weighted_values(p, k0, tk)

    def steps(i, carry):
        for k in range(unroll):
            step(unroll * i + k)
        return carry

    assert n_full % unroll == 0
    lax.fori_loop(0, n_full // unroll, steps, 0)

    def finalize(qi, carry):
        q0 = pl.multiple_of(qi * tq, tq)
        acc = acc_sc[qi]
        o = acc[:, :LANES] / acc[:, LANES:]
        o = o[:tq] - lam * o[tq:]
        o = o * lax.rsqrt(jnp.mean(o * o, axis=-1, keepdims=True) + EPS)
        o = o * subln_ref[...] * (1.0 - lambda_init)
        gate = gate_ref[pl.ds(q0, tq), :].astype(f32)
        y_ref[pl.ds(q0, tq), :] = (o * (gate * jax.nn.sigmoid(gate))).astype(bf16)
        return carry

    lax.fori_loop(0, y_ref.shape[0] // tq, finalize, 0, unroll=2)


def _diff_attention(proj, lq1, lk1, lq2, lk2, subln, batch, seq, lambda_init, tq, tk):
    def slab(c0):
        return pl.BlockSpec((None, None, seq, LANES), lambda b, h: (b, c0 + h, 0, 0))

    def small(a):
        return pl.BlockSpec(a.shape, lambda b, h: (0,) * a.ndim)

    table = jnp.array(_full_tiles(seq, tq, tk), jnp.int32)
    n_full = table.shape[1]
    qi = jnp.arange(tq)[:, None]
    kj = jnp.arange(tk)[None, :]
    bias = jnp.stack([jnp.where(kj <= qi + o * tq, 0.0, NEG).astype(f32) for o in range(tk // tq)])
    smalls = [lq1, lk1, lq2, lk2, subln, bias]
    kern = functools.partial(_diff_attn_kernel, tq=tq, tk=tk, n_full=n_full, unroll=28,
                             lambda_init=lambda_init)
    return pl.pallas_call(
        kern,
        grid=(batch, N_CHUNK),
        in_specs=([pl.BlockSpec(memory_space=pltpu.SMEM)] + [small(a) for a in smalls]
                  + [slab(0), slab(N_CHUNK), slab(2 * N_CHUNK), slab(3 * N_CHUNK)]),
        out_specs=pl.BlockSpec((None, seq, LANES), lambda b, h: (b, 0, h)),
        out_shape=jax.ShapeDtypeStruct((batch, seq, D_MODEL), bf16),
        scratch_shapes=[pltpu.VMEM((seq // tq, 2 * tq, 2 * LANES), f32),
                        pltpu.VMEM((seq // tq, 2 * tq, LANES), f32)],
        compiler_params=_cparams(2),
        name="diff_attention",
    )(table, *smalls, proj, proj, proj, proj)


def _residual_update(y_ref, w_ref, g_ref, h_ref):
    z = jnp.dot(y_ref[...], w_ref[...], preferred_element_type=f32)
    z = z * lax.rsqrt(jnp.mean(z * z, axis=-1, keepdims=True) + EPS)
    return h_ref[...] + z * g_ref[...]


def _out_proj_kernel(y_ref, w_ref, g_ref, h_ref, o_ref):
    o_ref[...] = _residual_update(y_ref, w_ref, g_ref, h_ref)


def _out_proj_norm_kernel(y_ref, w_ref, g_ref, h_ref, g_next_ref, o_ref, u_ref):
    h = _residual_update(y_ref, w_ref, g_ref, h_ref)
    o_ref[...] = h
    u_ref[...] = ((h * lax.rsqrt(jnp.mean(h * h, axis=-1, keepdims=True) + EPS)) * g_next_ref[...]).astype(bf16)


def _out_proj(y2d, w_bf16, gain, h2d, next_gain=None, bm=1024):
    m, d = h2d.shape
    row = pl.BlockSpec((bm, d), lambda i: (i, 0))
    vec = pl.BlockSpec((1, d), lambda i: (0, 0))
    in_specs = [row, pl.BlockSpec((d, d), lambda i: (0, 0)), vec, row]
    if next_gain is None:
        return pl.pallas_call(
            _out_proj_kernel, grid=(m // bm,), in_specs=in_specs, out_specs=row,
            out_shape=jax.ShapeDtypeStruct((m, d), f32),
            compiler_params=_cparams(1), name="out_proj",
        )(y2d, w_bf16, gain, h2d)
    return pl.pallas_call(
        _out_proj_norm_kernel, grid=(m // bm,), in_specs=in_specs + [vec],
        out_specs=[row, pl.BlockSpec((None, bm, d), lambda i: (0, i, 0))],
        out_shape=[jax.ShapeDtypeStruct((m, d), f32), jax.ShapeDtypeStruct((1, m, d), bf16)],
        compiler_params=_cparams(1), name="out_proj_norm",
    )(y2d, w_bf16, gain, h2d, next_gain)


def _rope_tables(seq, dils):
    half = HEAD_DIM // 2
    freqs = ROPE_THETA ** (-jnp.arange(0, HEAD_DIM, 2, dtype=f32) / HEAD_DIM)
    lane = jnp.arange(LANES)
    freq_l = freqs[lane % half][None, None, :]
    sign_l = jnp.where(lane % HEAD_DIM < half, -1.0, 1.0).astype(f32)[None, None, :]
    row = jnp.arange(seq)[None, :]
    d = jnp.array(dils)[:, None]
    w = row % ROW_TILE
    pos = row - w + (w % (ROW_TILE // d)) * d + w // (ROW_TILE // d)
    ang = pos.astype(f32)[:, :, None] * freq_l
    return jnp.cos(ang).astype(bf16), (sign_l * jnp.sin(ang)).astype(bf16)


def _window_bias():
    qi = jnp.arange(2 * WIN)[:, None] % WIN
    kj = jnp.arange(2 * WIN)[None, :]
    cur = (kj >= WIN) & (kj - WIN <= qi)
    prev = (kj < WIN) & (kj >= qi)
    return jnp.stack([jnp.where(cur, 0.0, NEG), jnp.where(cur | prev, 0.0, NEG)]).astype(f32)


def kernel(x, norm_pre, norm_post, dil_w_in, dil_w_out, diff_w_in, diff_w_out, diff_lambda_q1, diff_lambda_k1, diff_lambda_q2, diff_lambda_k2, diff_subln):
    batch, seq, d_model = x.shape
    assert d_model == D_MODEL and seq % ROW_TILE == 0
    n_g = len(DILATIONS)
    h = x.reshape(batch * seq, d_model)

    names = [kind + str(g) for kind in "qkv" for g in range(n_g)] + ["gate"]
    group = [0 if name == "gate" else int(name[1]) for name in names]
    visit = sorted(range(len(names)), key=lambda s: (group[s], s))
    sec_of = {name: s for s, name in enumerate(names)}
    sec_dsel = [group[s] for s in visit]
    rope_steps = [j for j, s in enumerate(visit) if names[s][0] in "qk"]
    cos_t, sin_t = _rope_tables(seq, DILATIONS)
    u = _prenorm(h, norm_pre[0][None, :], DILATIONS)
    proj = _in_proj(u, dil_w_in[0].astype(bf16), cos_t, sin_t, batch, seq, visit, sec_dsel, rope_steps)
    y = _dil_attention(proj, _window_bias(), batch, seq, sec_of)
    h, u = _out_proj(y.reshape(batch * seq, d_model), dil_w_out[0].astype(bf16), norm_post[0][None, :], h,
                     next_gain=norm_pre[1][None, :])

    lambda_init = 0.8 - 0.6 * math.exp(-0.3 * 1)
    tq, tk = 256, 512
    proj = _in_proj(u, diff_w_in[0].astype(bf16), cos_t[:1], sin_t[:1], batch, seq,
                    [0, 1, 2, 3], [0, 0, 0, 0], [0, 1])
    y = _diff_attention(proj, diff_lambda_q1[0][None, :], diff_lambda_k1[0][None, :],
                        diff_lambda_q2[0][None, :], diff_lambda_k2[0][None, :],
                        diff_subln[0][None, :], batch, seq, lambda_init, tq, tk)
    h = _out_proj(y.reshape(batch * seq, d_model), diff_w_out[0].astype(bf16), norm_post[1][None, :], h)
    return h.reshape(batch, seq, d_model)
```

```python
import functools
import math

import jax
import jax.numpy as jnp
from jax import lax
from jax.experimental import pallas as pl
from jax.experimental.pallas import tpu as pltpu

D_MODEL = 1024
HEAD_DIM = 64
ROPE_THETA = 10000.0
EPS = 1e-6
DILATIONS = (1, 4, 16)
WIN = 128
LANES = 128
N_CHUNK = D_MODEL // LANES
ROW_TILE = 2048
BLK_PER_TILE = ROW_TILE // WIN
NEG = -1e30
Q_SCALE = HEAD_DIM ** -0.5 * math.log2(math.e)
ROPE_K, ROPE_Q, N_ROPE_KINDS = 0, 1, 2
VMEM_LIMIT = 52 * 1024 * 1024

f32 = jnp.float32
bf16 = jnp.bfloat16


def _cparams(n_axes):
    return pltpu.CompilerParams(
        dimension_semantics=("arbitrary",) * n_axes, vmem_limit_bytes=VMEM_LIMIT)


def _prenorm_kernel(x_ref, g_ref, o_ref, xn_sc, rs_sc, xp_sc, rsp_sc, *, dils):
    k = pl.program_id(1)
    d_model = N_CHUNK * LANES
    sub = 2 * WIN

    @pl.when(k == 0)
    def _():
        def scale(i, carry):
            rows = pl.ds(pl.multiple_of(i * sub, sub), sub)
            sq = jnp.zeros((sub, LANES), f32)
            for c in range(N_CHUNK):
                x = x_ref[rows, c * LANES:(c + 1) * LANES]
                xn_sc[c, rows, :] = x
                sq = sq + x * x
            ss = jnp.sum(sq, axis=-1, keepdims=True)
            rs_sc[rows, :] = jnp.broadcast_to(lax.rsqrt(ss * (1.0 / d_model) + EPS), (sub, LANES))
            return carry

        lax.fori_loop(0, ROW_TILE // sub, scale, 0)

    def emit(kk):
        d = dils[kk]
        d_prev = dils[kk - 1] if kk else 1
        q = d // d_prev
        from_copy = d_prev > 1
        keep_copy = kk + 1 < len(dils) and d > 1
        per_res = BLK_PER_TILE // d

        def body(s, carry):
            r, n = s // per_res, s % per_res
            src = (r % d_prev) * (ROW_TILE // d_prev) + r // d_prev + q * n * WIN
            rows = pl.ds(src, WIN, stride=q) if q > 1 else pl.ds(pl.multiple_of(src, WIN), WIN)
            dst = pl.ds(pl.multiple_of(s * WIN, WIN), WIN)
            rs = (rsp_sc if from_copy else rs_sc)[rows, :]
            if keep_copy:
                rsp_sc[dst, :] = rs
            for c in range(N_CHUNK):
                x = (xp_sc if from_copy else xn_sc)[c, rows, :]
                if keep_copy:
                    xp_sc[c, dst, :] = x
                cols = slice(c * LANES, (c + 1) * LANES)
                o_ref[dst, cols] = ((x * rs) * g_ref[:, cols]).astype(bf16)
            return carry

        lax.fori_loop(0, BLK_PER_TILE, body, 0)

    assert all(d % p == 0 for p, d in zip(dils, dils[1:])) and sum(d > 1 for d in dils[:-1]) <= 1
    for kk in range(len(dils)):
        pl.when(k == kk)(functools.partial(emit, kk))


def _prenorm(x2d, gain, dils):
    m, d_model = x2d.shape
    slabs = pltpu.VMEM((N_CHUNK, ROW_TILE, LANES), f32)
    per_row = pltpu.VMEM((ROW_TILE, LANES), f32)
    return pl.pallas_call(
        functools.partial(_prenorm_kernel, dils=dils),
        grid=(m // ROW_TILE, len(dils)),
        in_specs=[pl.BlockSpec((ROW_TILE, d_model), lambda t, k: (t, 0)),
                  pl.BlockSpec((1, d_model), lambda t, k: (0, 0))],
        out_specs=pl.BlockSpec((None, ROW_TILE, d_model), lambda t, k: (k, t, 0)),
        out_shape=jax.ShapeDtypeStruct((len(dils), m, d_model), bf16),
        scratch_shapes=[slabs, per_row, slabs, per_row],
        compiler_params=_cparams(2),
        name="prenorm",
    )(x2d, gain)


def _proj_kernel(u_ref, w_ref, cos_ref, sin_ref, o_ref, *, rope_secs):
    j = pl.program_id(1)
    n_c = o_ref.shape[0]
    lane = lax.broadcasted_iota(jnp.int32, (1, LANES), 1)
    first_half = (lane % HEAD_DIM) < (HEAD_DIM // 2)
    is_rope = functools.reduce(jnp.logical_or, [j == s for s in rope_secs])

    @pl.when(is_rope)
    def _():
        cos, sin = cos_ref[...], sin_ref[...]
        half = n_c // 2
        for part in range(2):
            cols = slice(part * half * LANES, (part + 1) * half * LANES)
            acc = jnp.dot(u_ref[...], w_ref[:, cols], preferred_element_type=f32)
            for c in range(half):
                x = acc[:, c * LANES:(c + 1) * LANES].astype(bf16)
                partner = jnp.where(first_half,
                                    pltpu.roll(x, LANES - HEAD_DIM // 2, axis=1),
                                    pltpu.roll(x, HEAD_DIM // 2, axis=1))
                o_ref[part * half + c] = x * cos + partner * sin

    @pl.when(jnp.logical_not(is_rope))
    def _():
        acc = jnp.dot(u_ref[...], w_ref[...], preferred_element_type=f32)
        for c in range(n_c):
            o_ref[c] = acc[:, c * LANES:(c + 1) * LANES].astype(bf16)


def _lookup(j, table):
    return sum([jnp.where(j == pos, val, 0) for pos, val in enumerate(table)], jnp.int32(0))


def _in_proj(u, w_bf16, cos_t, sin_t, batch, seq, visit, sec_dsel, q_steps, k_steps, bm=ROW_TILE):
    _, m, d_in = u.shape
    n_out = w_bf16.shape[1]
    bn = D_MODEL
    tiles_per_seq = seq // bm
    sec_tab = [N_ROPE_KINDS * d + (ROPE_Q if j in q_steps else ROPE_K) for j, d in enumerate(sec_dsel)]

    def sec(j):
        return _lookup(j, visit)

    def dsel(j):
        return _lookup(j, sec_dsel)

    def tsel(j):
        return _lookup(j, sec_tab)

    kern = functools.partial(_proj_kernel, rope_secs=sorted(q_steps + k_steps))
    return pl.pallas_call(
        kern,
        grid=(m // bm, n_out // bn),
        in_specs=[
            pl.BlockSpec((None, bm, d_in), lambda i, j: (dsel(j), i, 0)),
            pl.BlockSpec((d_in, bn), lambda i, j: (0, sec(j))),
            pl.BlockSpec((None, bm, LANES), lambda i, j: (tsel(j), i % tiles_per_seq, 0)),
            pl.BlockSpec((None, bm, LANES), lambda i, j: (tsel(j), i % tiles_per_seq, 0)),
        ],
        out_specs=pl.BlockSpec((None, bn // LANES, bm, LANES),
                               lambda i, j: (i // tiles_per_seq, sec(j), i % tiles_per_seq, 0)),
        out_shape=jax.ShapeDtypeStruct((batch, n_out // LANES, seq, LANES), bf16),
        compiler_params=_cparams(2),
        name="in_proj",
    )(u, w_bf16, cos_t, sin_t)


def _dil_attn_kernel(bias_ref, q0, k0, v0, q1, k1, v1, q2, k2, v2, gate_ref, y_ref,
                     num_sc, den_sc, max_sc):
    seq = y_ref.shape[0]
    n_blk = seq // WIN
    lane = lax.broadcasted_iota(jnp.int32, (1, LANES), 1)
    low = lane < HEAD_DIM
    sel0 = jnp.where(low, 1.0, 0.0).astype(bf16)
    sel1 = jnp.where(low, 0.0, 1.0).astype(bf16)
    ones = jnp.ones((2 * WIN, LANES), bf16)
    groups = ((q0, k0, v0), (q1, k1, v1), (q2, k2, v2))

    def block(blk):
        t, u = divmod(blk, BLK_PER_TILE)
        row0 = blk * WIN
        for g, d in enumerate(DILATIONS):
            q_ref, k_ref, v_ref = groups[g]
            per_res = BLK_PER_TILE // d
            r, n = divmod(u, per_res)
            prev = blk - 1 if n > 0 else blk - BLK_PER_TILE + per_res - 1
            has_prev = prev >= 0
            prow0 = max(prev, 0) * WIN
            nat0 = t * ROW_TILE + n * WIN * d + r

            q = q_ref[pl.ds(row0, WIN), :]
            qs = jnp.concatenate([q * sel0, q * sel1], axis=0)
            kcat = jnp.concatenate([k_ref[pl.ds(prow0, WIN), :], k_ref[pl.ds(row0, WIN), :]], axis=0)
            vcat = jnp.concatenate([v_ref[pl.ds(prow0, WIN), :], v_ref[pl.ds(row0, WIN), :]], axis=0)
            s = lax.dot_general(qs, kcat, (((1,), (1,)), ((), ())), preferred_element_type=f32)
            s = s + bias_ref[int(has_prev)]
            m = jnp.max(s, axis=-1, keepdims=True)
            p = jnp.exp2(s - m).astype(bf16)
            vext = jnp.concatenate([vcat, ones], axis=1)
            pv = jnp.dot(p, vext, preferred_element_type=f32)
            rows = pl.ds(nat0, WIN, stride=d) if d > 1 else pl.ds(nat0, WIN)
            num_sc[g, rows, :] = jnp.where(low, pv[:WIN, :LANES], pv[WIN:, :LANES])
            den_sc[g, rows, :] = jnp.where(low, pv[:WIN, LANES:], pv[WIN:, LANES:])
            max_sc[g, rows, :] = jnp.where(low, m[:WIN], m[WIN:])

    def merge(i):
        rows = pl.ds(i * WIN, WIN)
        m0, m1, m2 = max_sc[0, rows, :], max_sc[1, rows, :], max_sc[2, rows, :]
        mx = jnp.maximum(jnp.maximum(m0, m1), m2)
        e0, e1, e2 = jnp.exp2(m0 - mx), jnp.exp2(m1 - mx), jnp.exp2(m2 - mx)
        num = e0 * num_sc[0, rows, :] + e1 * num_sc[1, rows, :] + e2 * num_sc[2, rows, :]
        den = e0 * den_sc[0, rows, :] + e1 * den_sc[1, rows, :] + e2 * den_sc[2, rows, :]
        gate = gate_ref[rows, :].astype(f32)
        y_ref[rows, :] = ((num * gate) / (den * (1.0 + jnp.exp(-gate)))).astype(bf16)

    for blk in range(n_blk):
        block(blk)
        if blk >= BLK_PER_TILE:
            merge(blk - BLK_PER_TILE)
    for i in range(n_blk - BLK_PER_TILE, n_blk):
        merge(i)


def _dil_attention(proj, bias, batch, seq, sec_of):
    def slab(sec):
        return pl.BlockSpec((None, None, seq, LANES), lambda b, hp: (b, sec * N_CHUNK + hp, 0, 0))

    specs = [pl.BlockSpec((2, 2 * WIN, 2 * WIN), lambda b, hp: (0, 0, 0))]
    args = [bias]
    for g in range(len(DILATIONS)):
        for kind in "qkv":
            specs.append(slab(sec_of[kind + str(g)]))
            args.append(proj)
    specs.append(slab(sec_of["gate"]))
    args.append(proj)
    return pl.pallas_call(
        _dil_attn_kernel,
        grid=(batch, N_CHUNK),
        in_specs=specs,
        out_specs=pl.BlockSpec((None, seq, LANES), lambda b, hp: (b, 0, hp)),
        out_shape=jax.ShapeDtypeStruct((batch, seq, D_MODEL), bf16),
        scratch_shapes=[pltpu.VMEM((len(DILATIONS), seq, LANES), f32)] * 3,
        compiler_params=_cparams(2),
        name="dilated_attention",
    )(*args)


TAB_QI, TAB_KJ = range(2)


def _full_tiles(seq, tq, tk):
    pairs = [(qi, kj) for qi in range(seq // tq) for kj in range((qi * tq) // tk)]
    return list(zip(*pairs))


def _diff_attn_kernel(tab_ref, lq1_ref, lk1_ref, lq2_ref, lk2_ref, subln_ref, bias_ref,
                      q_ref, k_ref, v_ref, gate_ref, y_ref, acc_sc, m_sc, *,
                      tq, tk, n_full, unroll, lambda_init):
    lane = lax.broadcasted_iota(jnp.int32, (1, LANES), 1)
    low = lane < HEAD_DIM
    sel0 = jnp.where(low, 1.0, 0.0).astype(bf16)
    sel1 = jnp.where(low, 0.0, 1.0).astype(bf16)
    ones = jnp.ones((tk, LANES), bf16)
    lam = (jnp.exp(jnp.sum(lq1_ref[...] * lk1_ref[...], axis=-1, keepdims=True))
           - jnp.exp(jnp.sum(lq2_ref[...] * lk2_ref[...], axis=-1, keepdims=True)) + lambda_init)

    def scores(q0, k0, width):
        q = q_ref[pl.ds(q0, tq), :]
        qs = jnp.concatenate([q * sel0, q * sel1], axis=0)
        return lax.dot_general(qs, k_ref[pl.ds(k0, width), :], (((1,), (1,)), ((), ())),
                               preferred_element_type=f32)

    def weighted_values(p, k0, width):
        vext = jnp.concatenate([v_ref[pl.ds(k0, width), :], ones[:width]], axis=1)
        return jnp.dot(p, vext, preferred_element_type=f32)

    for qi in range(y_ref.shape[0] // tq):
        k0 = (qi * tq) // tk * tk
        width = (qi + 1) * tq - k0
        bias = bias_ref[(qi * tq - k0) // tq, :, :width]
        s = scores(qi * tq, k0, width) + jnp.tile(bias, (2, 1))
        m = jnp.max(s, axis=-1, keepdims=True)
        m_sc[qi] = jnp.broadcast_to(m, m_sc.shape[1:])
        acc_sc[qi] = weighted_values(jnp.exp2(s - m).astype(bf16), k0, width)

    def step(t):
        qi = tab_ref[TAB_QI, t]
        q0 = pl.multiple_of(qi * tq, tq)
        k0 = pl.multiple_of(tab_ref[TAB_KJ, t] * tk, tk)
        s = scores(q0, k0, tk)
        m_old = m_sc[qi]
        m_new = jnp.maximum(m_old, jnp.max(s, axis=-1, keepdims=True))
        alpha = jnp.exp2(m_old - m_new)
        m_sc[qi] = m_new
        p = jnp.exp2(s - jnp.tile(m_new, (1, tk // LANES))).astype(bf16)
        acc_sc[qi] = jnp.tile(alpha, (1, 2)) * acc_sc[qi] + weighted_values(p, k0, tk)

    def steps(i, carry):
        for k in range(unroll):
            step(unroll * i + k)
        return carry

    assert n_full % unroll == 0
    lax.fori_loop(0, n_full // unroll, steps, 0)

    def finalize(qi, carry):
        q0 = pl.multiple_of(qi * tq, tq)
        acc = acc_sc[qi]
        o = acc[:, :LANES] / acc[:, LANES:]
        o = o[:tq] - lam * o[tq:]
        o = o * lax.rsqrt(jnp.mean(o * o, axis=-1, keepdims=True) + EPS)
        o = o * subln_ref[...] * (1.0 - lambda_init)
        gate = gate_ref[pl.ds(q0, tq), :].astype(f32)
        y_ref[pl.ds(q0, tq), :] = (o * (gate * jax.nn.sigmoid(gate))).astype(bf16)
        return carry

    lax.fori_loop(0, y_ref.shape[0] // tq, finalize, 0, unroll=2)


def _diff_attention(proj, lq1, lk1, lq2, lk2, subln, batch, seq, lambda_init, tq, tk):
    def slab(c0):
        return pl.BlockSpec((None, None, seq, LANES), lambda b, h: (b, c0 + h, 0, 0))

    def small(a):
        return pl.BlockSpec(a.shape, lambda b, h: (0,) * a.ndim)

    table = jnp.array(_full_tiles(seq, tq, tk), jnp.int32)
    n_full = table.shape[1]
    qi = jnp.arange(tq)[:, None]
    kj = jnp.arange(tk)[None, :]
    bias = jnp.stack([jnp.where(kj <= qi + o * tq, 0.0, NEG).astype(f32) for o in range(tk // tq)])
    smalls = [lq1, lk1, lq2, lk2, subln, bias]
    kern = functools.partial(_diff_attn_kernel, tq=tq, tk=tk, n_full=n_full, unroll=28,
                             lambda_init=lambda_init)
    return pl.pallas_call(
        kern,
        grid=(batch, N_CHUNK),
        in_specs=([pl.BlockSpec(memory_space=pltpu.SMEM)] + [small(a) for a in smalls]
                  + [slab(0), slab(N_CHUNK), slab(2 * N_CHUNK), slab(3 * N_CHUNK)]),
        out_specs=pl.BlockSpec((None, seq, LANES), lambda b, h: (b, 0, h)),
        out_shape=jax.ShapeDtypeStruct((batch, seq, D_MODEL), bf16),
        scratch_shapes=[pltpu.VMEM((seq // tq, 2 * tq, 2 * LANES), f32),
                        pltpu.VMEM((seq // tq, 2 * tq, LANES), f32)],
        compiler_params=_cparams(2),
        name="diff_attention",
    )(table, *smalls, proj, proj, proj, proj)


def _residual_update(y_ref, w_ref, g_ref, h_ref):
    z = jnp.dot(y_ref[...], w_ref[...], preferred_element_type=f32)
    z = z * lax.rsqrt(jnp.mean(z * z, axis=-1, keepdims=True) + EPS)
    return h_ref[...] + z * g_ref[...]


def _out_proj_kernel(y_ref, w_ref, g_ref, h_ref, o_ref):
    o_ref[...] = _residual_update(y_ref, w_ref, g_ref, h_ref)


def _out_proj_norm_kernel(y_ref, w_ref, g_ref, h_ref, g_next_ref, o_ref, u_ref):
    h = _residual_update(y_ref, w_ref, g_ref, h_ref)
    o_ref[...] = h
    u_ref[...] = ((h * lax.rsqrt(jnp.mean(h * h, axis=-1, keepdims=True) + EPS)) * g_next_ref[...]).astype(bf16)


def _out_proj(y2d, w_bf16, gain, h2d, next_gain=None, bm=1024):
    m, d = h2d.shape
    row = pl.BlockSpec((bm, d), lambda i: (i, 0))
    vec = pl.BlockSpec((1, d), lambda i: (0, 0))
    in_specs = [row, pl.BlockSpec((d, d), lambda i: (0, 0)), vec, row]
    if next_gain is None:
        return pl.pallas_call(
            _out_proj_kernel, grid=(m // bm,), in_specs=in_specs, out_specs=row,
            out_shape=jax.ShapeDtypeStruct((m, d), f32),
            compiler_params=_cparams(1), name="out_proj",
        )(y2d, w_bf16, gain, h2d)
    return pl.pallas_call(
        _out_proj_norm_kernel, grid=(m // bm,), in_specs=in_specs + [vec],
        out_specs=[row, pl.BlockSpec((None, bm, d), lambda i: (0, i, 0))],
        out_shape=[jax.ShapeDtypeStruct((m, d), f32), jax.ShapeDtypeStruct((1, m, d), bf16)],
        compiler_params=_cparams(1), name="out_proj_norm",
    )(y2d, w_bf16, gain, h2d, next_gain)


def _rope_tables(seq, dils):
    half = HEAD_DIM // 2
    freqs = ROPE_THETA ** (-jnp.arange(0, HEAD_DIM, 2, dtype=f32) / HEAD_DIM)
    lane = jnp.arange(LANES)
    freq_l = freqs[lane % half][None, None, :]
    sign_l = jnp.where(lane % HEAD_DIM < half, -1.0, 1.0).astype(f32)[None, None, :]
    row = jnp.arange(seq)[None, :]
    d = jnp.array(dils)[:, None]
    w = row % ROW_TILE
    pos = row - w + (w % (ROW_TILE // d)) * d + w // (ROW_TILE // d)
    ang = pos.astype(f32)[:, None, :, None] * freq_l
    kind_scale = jnp.array([1.0, Q_SCALE], f32)[None, :, None, None]
    cos = (jnp.cos(ang) * kind_scale).astype(bf16)
    sin = (sign_l * jnp.sin(ang) * kind_scale).astype(bf16)
    return cos.reshape(-1, seq, LANES), sin.reshape(-1, seq, LANES)


def _window_bias():
    qi = jnp.arange(2 * WIN)[:, None] % WIN
    kj = jnp.arange(2 * WIN)[None, :]
    cur = (kj >= WIN) & (kj - WIN <= qi)
    prev = (kj < WIN) & (kj >= qi)
    return jnp.stack([jnp.where(cur, 0.0, NEG), jnp.where(cur | prev, 0.0, NEG)]).astype(f32)


def kernel(x, norm_pre, norm_post, dil_w_in, dil_w_out, diff_w_in, diff_w_out, diff_lambda_q1, diff_lambda_k1, diff_lambda_q2, diff_lambda_k2, diff_subln):
    batch, seq, d_model = x.shape
    assert d_model == D_MODEL and seq % ROW_TILE == 0
    n_g = len(DILATIONS)
    h = x.reshape(batch * seq, d_model)

    names = [kind + str(g) for kind in "qkv" for g in range(n_g)] + ["gate"]
    group = [0 if name == "gate" else int(name[1]) for name in names]
    visit = sorted(range(len(names)), key=lambda s: (group[s], s))
    sec_of = {name: s for s, name in enumerate(names)}
    sec_dsel = [group[s] for s in visit]
    q_steps = [j for j, s in enumerate(visit) if names[s][0] == "q"]
    k_steps = [j for j, s in enumerate(visit) if names[s][0] == "k"]
    cos_t, sin_t = _rope_tables(seq, DILATIONS)
    u = _prenorm(h, norm_pre[0][None, :], DILATIONS)
    proj = _in_proj(u, dil_w_in[0].astype(bf16), cos_t, sin_t, batch, seq, visit, sec_dsel, q_steps, k_steps)
    y = _dil_attention(proj, _window_bias(), batch, seq, sec_of)
    h, u = _out_proj(y.reshape(batch * seq, d_model), dil_w_out[0].astype(bf16), norm_post[0][None, :], h,
                     next_gain=norm_pre[1][None, :])

    lambda_init = 0.8 - 0.6 * math.exp(-0.3 * 1)
    tq, tk = 256, 512
    proj = _in_proj(u, diff_w_in[0].astype(bf16), cos_t[:N_ROPE_KINDS], sin_t[:N_ROPE_KINDS], batch, seq,
                    [0, 1, 2, 3], [0, 0, 0, 0], [0], [1])
    y = _diff_attention(proj, diff_lambda_q1[0][None, :], diff_lambda_k1[0][None, :],
                        diff_lambda_q2[0][None, :], diff_lambda_k2[0][None, :],
                        diff_subln[0][None, :], batch, seq, lambda_init, tq, tk)
    h = _out_proj(y.reshape(batch * seq, d_model), diff_w_out[0].astype(bf16), norm_post[1][None, :], h)
    return h.reshape(batch, seq, d_model)
```

```python
import functools
import math

import jax
import jax.numpy as jnp
from jax import lax
from jax.experimental import pallas as pl
from jax.experimental.pallas import tpu as pltpu

D_MODEL = 1024
HEAD_DIM = 64
ROPE_THETA = 10000.0
EPS = 1e-6
DILATIONS = (1, 4, 16)
WIN = 128
LANES = 128
N_CHUNK = D_MODEL // LANES
ROW_TILE = 2048
BLK_PER_TILE = ROW_TILE // WIN
NEG = -1e30
Q_SCALE = HEAD_DIM ** -0.5 * math.log2(math.e)
ROPE_K, ROPE_Q, N_ROPE_KINDS = 0, 1, 2
VMEM_LIMIT = 52 * 1024 * 1024

f32 = jnp.float32
bf16 = jnp.bfloat16


def _cparams(n_axes):
    return pltpu.CompilerParams(
        dimension_semantics=("arbitrary",) * n_axes, vmem_limit_bytes=VMEM_LIMIT)


def _prenorm_kernel(x_ref, g_ref, o_ref, xn_sc, rs_sc, xp_sc, rsp_sc, *, dils):
    k = pl.program_id(1)
    d_model = N_CHUNK * LANES
    sub = 2 * WIN

    @pl.when(k == 0)
    def _():
        def scale(i, carry):
            rows = pl.ds(pl.multiple_of(i * sub, sub), sub)
            sq = jnp.zeros((sub, LANES), f32)
            for c in range(N_CHUNK):
                x = x_ref[rows, c * LANES:(c + 1) * LANES]
                xn_sc[c, rows, :] = x
                sq = sq + x * x
            ss = jnp.sum(sq, axis=-1, keepdims=True)
            rs_sc[rows, :] = jnp.broadcast_to(lax.rsqrt(ss * (1.0 / d_model) + EPS), (sub, LANES))
            return carry

        lax.fori_loop(0, ROW_TILE // sub, scale, 0)

    def emit(kk):
        d = dils[kk]
        d_prev = dils[kk - 1] if kk else 1
        q = d // d_prev
        from_copy = d_prev > 1
        keep_copy = kk + 1 < len(dils) and d > 1
        per_res = BLK_PER_TILE // d

        def body(s, carry):
            r, n = s // per_res, s % per_res
            src = (r % d_prev) * (ROW_TILE // d_prev) + r // d_prev + q * n * WIN
            rows = pl.ds(src, WIN, stride=q) if q > 1 else pl.ds(pl.multiple_of(src, WIN), WIN)
            dst = pl.ds(pl.multiple_of(s * WIN, WIN), WIN)
            rs = (rsp_sc if from_copy else rs_sc)[rows, :]
            if keep_copy:
                rsp_sc[dst, :] = rs
            for c in range(N_CHUNK):
                x = (xp_sc if from_copy else xn_sc)[c, rows, :]
                if keep_copy:
                    xp_sc[c, dst, :] = x
                cols = slice(c * LANES, (c + 1) * LANES)
                o_ref[dst, cols] = ((x * rs) * g_ref[:, cols]).astype(bf16)
            return carry

        lax.fori_loop(0, BLK_PER_TILE, body, 0)

    assert all(d % p == 0 for p, d in zip(dils, dils[1:])) and sum(d > 1 for d in dils[:-1]) <= 1
    for kk in range(len(dils)):
        pl.when(k == kk)(functools.partial(emit, kk))


def _prenorm(x2d, gain, dils):
    m, d_model = x2d.shape
    slabs = pltpu.VMEM((N_CHUNK, ROW_TILE, LANES), f32)
    per_row = pltpu.VMEM((ROW_TILE, LANES), f32)
    return pl.pallas_call(
        functools.partial(_prenorm_kernel, dils=dils),
        grid=(m // ROW_TILE, len(dils)),
        in_specs=[pl.BlockSpec((ROW_TILE, d_model), lambda t, k: (t, 0)),
                  pl.BlockSpec((1, d_model), lambda t, k: (0, 0))],
        out_specs=pl.BlockSpec((None, ROW_TILE, d_model), lambda t, k: (k, t, 0)),
        out_shape=jax.ShapeDtypeStruct((len(dils), m, d_model), bf16),
        scratch_shapes=[slabs, per_row, slabs, per_row],
        compiler_params=_cparams(2),
        name="prenorm",
    )(x2d, gain)


def _proj_kernel(u_ref, w_ref, cos_ref, sin_ref, o_ref, *, rope_secs):
    j = pl.program_id(1)
    n_c = o_ref.shape[0]
    lane = lax.broadcasted_iota(jnp.int32, (1, LANES), 1)
    first_half = (lane % HEAD_DIM) < (HEAD_DIM // 2)
    is_rope = functools.reduce(jnp.logical_or, [j == s for s in rope_secs])

    @pl.when(is_rope)
    def _():
        cos, sin = cos_ref[...], sin_ref[...]
        half = n_c // 2
        for part in range(2):
            cols = slice(part * half * LANES, (part + 1) * half * LANES)
            acc = jnp.dot(u_ref[...], w_ref[:, cols], preferred_element_type=f32)
            for c in range(half):
                x = acc[:, c * LANES:(c + 1) * LANES].astype(bf16)
                partner = jnp.where(first_half,
                                    pltpu.roll(x, LANES - HEAD_DIM // 2, axis=1),
                                    pltpu.roll(x, HEAD_DIM // 2, axis=1))
                o_ref[part * half + c] = x * cos + partner * sin

    @pl.when(jnp.logical_not(is_rope))
    def _():
        acc = jnp.dot(u_ref[...], w_ref[...], preferred_element_type=f32)
        for c in range(n_c):
            o_ref[c] = acc[:, c * LANES:(c + 1) * LANES].astype(bf16)


def _lookup(j, table):
    return sum([jnp.where(j == pos, val, 0) for pos, val in enumerate(table)], jnp.int32(0))


def _in_proj(u, w_bf16, cos_t, sin_t, batch, seq, visit, sec_dsel, q_steps, k_steps, bm=ROW_TILE):
    _, m, d_in = u.shape
    n_out = w_bf16.shape[1]
    bn = D_MODEL
    tiles_per_seq = seq // bm
    sec_tab = [N_ROPE_KINDS * d + (ROPE_Q if j in q_steps else ROPE_K) for j, d in enumerate(sec_dsel)]

    def sec(j):
        return _lookup(j, visit)

    def dsel(j):
        return _lookup(j, sec_dsel)

    def tsel(j):
        return _lookup(j, sec_tab)

    kern = functools.partial(_proj_kernel, rope_secs=sorted(q_steps + k_steps))
    return pl.pallas_call(
        kern,
        grid=(m // bm, n_out // bn),
        in_specs=[
            pl.BlockSpec((None, bm, d_in), lambda i, j: (dsel(j), i, 0)),
            pl.BlockSpec((d_in, bn), lambda i, j: (0, sec(j))),
            pl.BlockSpec((None, bm, LANES), lambda i, j: (tsel(j), i % tiles_per_seq, 0)),
            pl.BlockSpec((None, bm, LANES), lambda i, j: (tsel(j), i % tiles_per_seq, 0)),
        ],
        out_specs=pl.BlockSpec((None, bn // LANES, bm, LANES),
                               lambda i, j: (i // tiles_per_seq, sec(j), i % tiles_per_seq, 0)),
        out_shape=jax.ShapeDtypeStruct((batch, n_out // LANES, seq, LANES), bf16),
        compiler_params=_cparams(2),
        name="in_proj",
    )(u, w_bf16, cos_t, sin_t)


def _dil_attn_kernel(bias_ref, q0, k0, v0, q1, k1, v1, q2, k2, v2, gate_ref, y_ref,
                     num_sc, den_sc, max_sc, num_stage, den_stage, max_stage):
    seq = y_ref.shape[0]
    n_blk = seq // WIN
    lane = lax.broadcasted_iota(jnp.int32, (1, LANES), 1)
    low = lane < HEAD_DIM
    quarter = ROW_TILE // 4
    outs = (num_sc, den_sc, max_sc)
    stages = (num_stage, den_stage, max_stage)
    sel0 = jnp.where(low, 1.0, 0.0).astype(bf16)
    sel1 = jnp.where(low, 0.0, 1.0).astype(bf16)
    ones = jnp.ones((2 * WIN, LANES), bf16)
    groups = ((q0, k0, v0), (q1, k1, v1), (q2, k2, v2))

    def block(blk):
        t, u = divmod(blk, BLK_PER_TILE)
        row0 = blk * WIN
        for g, d in enumerate(DILATIONS):
            q_ref, k_ref, v_ref = groups[g]
            per_res = BLK_PER_TILE // d
            r, n = divmod(u, per_res)
            prev = blk - 1 if n > 0 else blk - BLK_PER_TILE + per_res - 1
            has_prev = prev >= 0
            prow0 = max(prev, 0) * WIN
            nat0 = t * ROW_TILE + n * WIN * d + r

            q = q_ref[pl.ds(row0, WIN), :]
            qs = jnp.concatenate([q * sel0, q * sel1], axis=0)
            kcat = jnp.concatenate([k_ref[pl.ds(prow0, WIN), :], k_ref[pl.ds(row0, WIN), :]], axis=0)
            vcat = jnp.concatenate([v_ref[pl.ds(prow0, WIN), :], v_ref[pl.ds(row0, WIN), :]], axis=0)
            s = lax.dot_general(qs, kcat, (((1,), (1,)), ((), ())), preferred_element_type=f32)
            s = s + bias_ref[int(has_prev)]
            m = jnp.max(s, axis=-1, keepdims=True)
            p = jnp.exp2(s - m).astype(bf16)
            vext = jnp.concatenate([vcat, ones], axis=1)
            pv = jnp.dot(p, vext, preferred_element_type=f32)
            vals = (jnp.where(low, pv[:WIN, :LANES], pv[WIN:, :LANES]),
                    jnp.where(low, pv[:WIN, LANES:], pv[WIN:, LANES:]),
                    jnp.where(low, m[:WIN], m[WIN:]))
            if d <= 4:
                rows = pl.ds(nat0, WIN, stride=d) if d > 1 else pl.ds(nat0, WIN)
                for out, val in zip(outs, vals):
                    out[g, rows, :] = val
            else:
                a, b = divmod(r, 4)
                rows = pl.ds(t * ROW_TILE + b * quarter + (d // 4) * n * WIN + a, WIN, stride=d // 4)
                for stage, val in zip(stages, vals):
                    stage[rows, :] = val

    def unstage(t):
        g = [d > 4 for d in DILATIONS].index(True)
        for b in range(4):
            for c in range(quarter // WIN):
                src = pl.ds(t * ROW_TILE + b * quarter + c * WIN, WIN)
                dst = pl.ds(t * ROW_TILE + 4 * c * WIN + b, WIN, stride=4)
                for out, stage in zip(outs, stages):
                    out[g, dst, :] = stage[src, :]

    def merge(i):
        rows = pl.ds(i * WIN, WIN)
        maxs = [max_sc[g, rows, :] for g in range(len(DILATIONS))]
        mx = functools.reduce(jnp.maximum, maxs)
        es = [jnp.exp2(m - mx) for m in maxs]
        num = sum(e * num_sc[g, rows, :] for g, e in enumerate(es))
        den = sum(e * den_sc[g, rows, :] for g, e in enumerate(es))
        gate = gate_ref[rows, :].astype(f32)
        y_ref[rows, :] = ((num * gate) / (den * (1.0 + jnp.exp(-gate)))).astype(bf16)

    assert sum(d > 4 for d in DILATIONS) == 1 and all(d <= 16 for d in DILATIONS)
    for blk in range(n_blk):
        block(blk)
        if blk % BLK_PER_TILE == BLK_PER_TILE - 1:
            unstage(blk // BLK_PER_TILE)
        if blk >= BLK_PER_TILE:
            merge(blk - BLK_PER_TILE)
    for i in range(n_blk - BLK_PER_TILE, n_blk):
        merge(i)


def _dil_attention(proj, batch, seq, sec_of):
    def slab(sec):
        return pl.BlockSpec((None, None, seq, LANES), lambda b, hp: (b, sec * N_CHUNK + hp, 0, 0))

    specs = [pl.BlockSpec((2, 2 * WIN, 2 * WIN), lambda b, hp: (0, 0, 0))]
    args = [_window_bias()]
    for g in range(len(DILATIONS)):
        for kind in "qkv":
            specs.append(slab(sec_of[kind + str(g)]))
            args.append(proj)
    specs.append(slab(sec_of["gate"]))
    args.append(proj)
    return pl.pallas_call(
        _dil_attn_kernel,
        grid=(batch, N_CHUNK),
        in_specs=specs,
        out_specs=pl.BlockSpec((None, seq, LANES), lambda b, hp: (b, 0, hp)),
        out_shape=jax.ShapeDtypeStruct((batch, seq, D_MODEL), bf16),
        scratch_shapes=[pltpu.VMEM((len(DILATIONS), seq, LANES), f32)] * 3 + [pltpu.VMEM((seq, LANES), f32)] * 3,
        compiler_params=_cparams(2),
        name="dilated_attention",
    )(*args)


TAB_QI, TAB_KJ = range(2)


def _full_tiles(seq, tq, tk):
    pairs = [(qi, kj) for qi in range(seq // tq) for kj in range((qi * tq) // tk)]
    return list(zip(*pairs))


def _diff_attn_kernel(tab_ref, lq1_ref, lk1_ref, lq2_ref, lk2_ref, subln_ref, bias_ref,
                      q_ref, k_ref, v_ref, gate_ref, y_ref, acc_sc, m_sc, *,
                      tq, tk, n_full, unroll, lambda_init):
    lane = lax.broadcasted_iota(jnp.int32, (1, LANES), 1)
    low = lane < HEAD_DIM
    sel0 = jnp.where(low, 1.0, 0.0).astype(bf16)
    sel1 = jnp.where(low, 0.0, 1.0).astype(bf16)
    ones = jnp.ones((tk, LANES), bf16)
    lam = (jnp.exp(jnp.sum(lq1_ref[...] * lk1_ref[...], axis=-1, keepdims=True))
           - jnp.exp(jnp.sum(lq2_ref[...] * lk2_ref[...], axis=-1, keepdims=True)) + lambda_init)

    def scores(q0, k0, width):
        q = q_ref[pl.ds(q0, tq), :]
        qs = jnp.concatenate([q * sel0, q * sel1], axis=0)
        return lax.dot_general(qs, k_ref[pl.ds(k0, width), :], (((1,), (1,)), ((), ())),
                               preferred_element_type=f32)

    def weighted_values(p, k0, width):
        vext = jnp.concatenate([v_ref[pl.ds(k0, width), :], ones[:width]], axis=1)
        return jnp.dot(p, vext, preferred_element_type=f32)

    for qi in range(y_ref.shape[0] // tq):
        k0 = (qi * tq) // tk * tk
        width = (qi + 1) * tq - k0
        bias = bias_ref[(qi * tq - k0) // tq, :, :width]
        s = scores(qi * tq, k0, width) + jnp.tile(bias, (2, 1))
        m = jnp.max(s, axis=-1, keepdims=True)
        m_sc[qi] = jnp.broadcast_to(m, m_sc.shape[1:])
        acc_sc[qi] = weighted_values(jnp.exp2(s - m).astype(bf16), k0, width)

    def step(t):
        qi = tab_ref[TAB_QI, t]
        q0 = pl.multiple_of(qi * tq, tq)
        k0 = pl.multiple_of(tab_ref[TAB_KJ, t] * tk, tk)
        s = scores(q0, k0, tk)
        m_old = m_sc[qi]
        m_new = jnp.maximum(m_old, jnp.max(s, axis=-1, keepdims=True))
        alpha = jnp.exp2(m_old - m_new)
        m_sc[qi] = m_new
        p = jnp.exp2(s - jnp.tile(m_new, (1, tk // LANES))).astype(bf16)
        acc_sc[qi] = jnp.tile(alpha, (1, 2)) * acc_sc[qi] + weighted_values(p, k0, tk)

    def steps(i, carry):
        for k in range(unroll):
            step(unroll * i + k)
        return carry

    assert n_full % unroll == 0
    lax.fori_loop(0, n_full // unroll, steps, 0)

    def finalize(qi, carry):
        q0 = pl.multiple_of(qi * tq, tq)
        acc = acc_sc[qi]
        o = acc[:, :LANES] / acc[:, LANES:]
        o = o[:tq] - lam * o[tq:]
        o = o * lax.rsqrt(jnp.mean(o * o, axis=-1, keepdims=True) + EPS)
        o = o * subln_ref[...] * (1.0 - lambda_init)
        gate = gate_ref[pl.ds(q0, tq), :].astype(f32)
        y_ref[pl.ds(q0, tq), :] = (o * (gate * jax.nn.sigmoid(gate))).astype(bf16)
        return carry

    lax.fori_loop(0, y_ref.shape[0] // tq, finalize, 0, unroll=2)


def _diff_attention(proj, lq1, lk1, lq2, lk2, subln, batch, seq, lambda_init, tq, tk):
    def slab(c0):
        return pl.BlockSpec((None, None, seq, LANES), lambda b, h: (b, c0 + h, 0, 0))

    def small(a):
        return pl.BlockSpec(a.shape, lambda b, h: (0,) * a.ndim)

    table = jnp.array(_full_tiles(seq, tq, tk), jnp.int32)
    n_full = table.shape[1]
    qi = jnp.arange(tq)[:, None]
    kj = jnp.arange(tk)[None, :]
    bias = jnp.stack([jnp.where(kj <= qi + o * tq, 0.0, NEG).astype(f32) for o in range(tk // tq)])
    smalls = [lq1, lk1, lq2, lk2, subln, bias]
    kern = functools.partial(_diff_attn_kernel, tq=tq, tk=tk, n_full=n_full, unroll=28,
                             lambda_init=lambda_init)
    return pl.pallas_call(
        kern,
        grid=(batch, N_CHUNK),
        in_specs=([pl.BlockSpec(memory_space=pltpu.SMEM)] + [small(a) for a in smalls]
                  + [slab(0), slab(N_CHUNK), slab(2 * N_CHUNK), slab(3 * N_CHUNK)]),
        out_specs=pl.BlockSpec((None, seq, LANES), lambda b, h: (b, 0, h)),
        out_shape=jax.ShapeDtypeStruct((batch, seq, D_MODEL), bf16),
        scratch_shapes=[pltpu.VMEM((seq // tq, 2 * tq, 2 * LANES), f32),
                        pltpu.VMEM((seq // tq, 2 * tq, LANES), f32)],
        compiler_params=_cparams(2),
        name="diff_attention",
    )(table, *smalls, proj, proj, proj, proj)


def _residual_update(y_ref, w_ref, g_ref, h_ref):
    z = jnp.dot(y_ref[...], w_ref[...], preferred_element_type=f32)
    z = z * lax.rsqrt(jnp.mean(z * z, axis=-1, keepdims=True) + EPS)
    return h_ref[...] + z * g_ref[...]


def _out_proj_kernel(y_ref, w_ref, g_ref, h_ref, o_ref):
    o_ref[...] = _residual_update(y_ref, w_ref, g_ref, h_ref)


def _out_proj_norm_kernel(y_ref, w_ref, g_ref, h_ref, g_next_ref, o_ref, u_ref):
    h = _residual_update(y_ref, w_ref, g_ref, h_ref)
    o_ref[...] = h
    u_ref[...] = ((h * lax.rsqrt(jnp.mean(h * h, axis=-1, keepdims=True) + EPS)) * g_next_ref[...]).astype(bf16)


def _out_proj(y2d, w_bf16, gain, h2d, next_gain=None, bm=1024):
    m, d = h2d.shape
    row = pl.BlockSpec((bm, d), lambda i: (i, 0))
    vec = pl.BlockSpec((1, d), lambda i: (0, 0))
    in_specs = [row, pl.BlockSpec((d, d), lambda i: (0, 0)), vec, row]
    if next_gain is None:
        return pl.pallas_call(
            _out_proj_kernel, grid=(m // bm,), in_specs=in_specs, out_specs=row,
            out_shape=jax.ShapeDtypeStruct((m, d), f32),
            compiler_params=_cparams(1), name="out_proj",
        )(y2d, w_bf16, gain, h2d)
    return pl.pallas_call(
        _out_proj_norm_kernel, grid=(m // bm,), in_specs=in_specs + [vec],
        out_specs=[row, pl.BlockSpec((None, bm, d), lambda i: (0, i, 0))],
        out_shape=[jax.ShapeDtypeStruct((m, d), f32), jax.ShapeDtypeStruct((1, m, d), bf16)],
        compiler_params=_cparams(1), name="out_proj_norm",
    )(y2d, w_bf16, gain, h2d, next_gain)


def _rope_tables(seq, dils):
    half = HEAD_DIM // 2
    freqs = ROPE_THETA ** (-jnp.arange(0, HEAD_DIM, 2, dtype=f32) / HEAD_DIM)
    lane = jnp.arange(LANES)
    freq_l = freqs[lane % half][None, None, :]
    sign_l = jnp.where(lane % HEAD_DIM < half, -1.0, 1.0).astype(f32)[None, None, :]
    row = jnp.arange(seq)[None, :]
    d = jnp.array(dils)[:, None]
    w = row % ROW_TILE
    pos = row - w + (w % (ROW_TILE // d)) * d + w // (ROW_TILE // d)
    ang = pos.astype(f32)[:, None, :, None] * freq_l
    kind_scale = jnp.array([1.0, Q_SCALE], f32)[None, :, None, None]
    cos = (jnp.cos(ang) * kind_scale).astype(bf16)
    sin = (sign_l * jnp.sin(ang) * kind_scale).astype(bf16)
    return cos.reshape(-1, seq, LANES), sin.reshape(-1, seq, LANES)


def _window_bias():
    qi = jnp.arange(2 * WIN)[:, None] % WIN
    kj = jnp.arange(2 * WIN)[None, :]
    cur = (kj >= WIN) & (kj - WIN <= qi)
    prev = (kj < WIN) & (kj >= qi)
    return jnp.stack([jnp.where(cur, 0.0, NEG), jnp.where(cur | prev, 0.0, NEG)]).astype(f32)


def kernel(x, norm_pre, norm_post, dil_w_in, dil_w_out, diff_w_in, diff_w_out, diff_lambda_q1, diff_lambda_k1, diff_lambda_q2, diff_lambda_k2, diff_subln):
    batch, seq, d_model = x.shape
    assert d_model == D_MODEL and seq % ROW_TILE == 0
    n_g = len(DILATIONS)
    h = x.reshape(batch * seq, d_model)

    names = [kind + str(g) for kind in "qkv" for g in range(n_g)] + ["gate"]
    group = [0 if name == "gate" else int(name[1]) for name in names]
    visit = sorted(range(len(names)), key=lambda s: (group[s], s))
    sec_of = {name: s for s, name in enumerate(names)}
    sec_dsel = [group[s] for s in visit]
    q_steps = [j for j, s in enumerate(visit) if names[s][0] == "q"]
    k_steps = [j for j, s in enumerate(visit) if names[s][0] == "k"]
    cos_t, sin_t = _rope_tables(seq, DILATIONS)
    u = _prenorm(h, norm_pre[0][None, :], DILATIONS)
    proj = _in_proj(u, dil_w_in[0].astype(bf16), cos_t, sin_t, batch, seq, visit, sec_dsel, q_steps, k_steps)
    y = _dil_attention(proj, batch, seq, sec_of)
    h, u = _out_proj(y.reshape(batch * seq, d_model), dil_w_out[0].astype(bf16), norm_post[0][None, :], h,
                     next_gain=norm_pre[1][None, :])

    lambda_init = 0.8 - 0.6 * math.exp(-0.3 * 1)
    tq, tk = 256, 512
    proj = _in_proj(u, diff_w_in[0].astype(bf16), cos_t[:N_ROPE_KINDS], sin_t[:N_ROPE_KINDS], batch, seq,
                    [0, 1, 2, 3], [0, 0, 0, 0], [0], [1])
    y = _diff_attention(proj, diff_lambda_q1[0][None, :], diff_lambda_k1[0][None, :],
                        diff_lambda_q2[0][None, :], diff_lambda_k2[0][None, :],
                        diff_subln[0][None, :], batch, seq, lambda_init, tq, tk)
    h = _out_proj(y.reshape(batch * seq, d_model), diff_w_out[0].astype(bf16), norm_post[1][None, :], h)
    return h.reshape(batch, seq, d_model)
```

```python
import functools
import math

import jax
import jax.numpy as jnp
from jax import lax
from jax.experimental import pallas as pl
from jax.experimental.pallas import tpu as pltpu

D_MODEL = 1024
HEAD_DIM = 64
ROPE_THETA = 10000.0
EPS = 1e-6
DILATIONS = (1, 4, 16)
WIN = 128
LANES = 128
N_CHUNK = D_MODEL // LANES
ROW_TILE = 2048
BLK_PER_TILE = ROW_TILE // WIN
NEG = -1e30
Q_SCALE = HEAD_DIM ** -0.5 * math.log2(math.e)
BLOCKWISE = 0
ORDERS = (BLOCKWISE, 4, 16)
MERGE_ORDER = 4
ROPE_K, ROPE_Q, N_ROPE_KINDS = 0, 1, 2
VMEM_LIMIT = 52 * 1024 * 1024

f32 = jnp.float32
bf16 = jnp.bfloat16


def _cparams(n_axes):
    return pltpu.CompilerParams(
        dimension_semantics=("arbitrary",) * n_axes, vmem_limit_bytes=VMEM_LIMIT)


def _prenorm_kernel(x_ref, g_ref, o_ref, xn_sc, rs_sc, xp_sc, rsp_sc, *, dils):
    k = pl.program_id(1)
    d_model = N_CHUNK * LANES
    sub = 2 * WIN

    @pl.when(k == 0)
    def _():
        def scale(i, carry):
            rows = pl.ds(pl.multiple_of(i * sub, sub), sub)
            sq = jnp.zeros((sub, LANES), f32)
            for c in range(N_CHUNK):
                x = x_ref[rows, c * LANES:(c + 1) * LANES]
                xn_sc[c, rows, :] = x
                sq = sq + x * x
            ss = jnp.sum(sq, axis=-1, keepdims=True)
            rs_sc[rows, :] = jnp.broadcast_to(lax.rsqrt(ss * (1.0 / d_model) + EPS), (sub, LANES))
            return carry

        lax.fori_loop(0, ROW_TILE // sub, scale, 0)

    def emit_blockwise():
        def body(s, carry):
            for r in range(4):
                rows = pl.ds(s * WIN + r, WIN // 4, stride=4)
                dst = pl.ds(pl.multiple_of(s * WIN + r * (WIN // 4), WIN // 4), WIN // 4)
                rs = rs_sc[rows, :]
                for c in range(N_CHUNK):
                    cols = slice(c * LANES, (c + 1) * LANES)
                    o_ref[dst, cols] = ((xn_sc[c, rows, :] * rs) * g_ref[:, cols]).astype(bf16)
            return carry

        lax.fori_loop(0, BLK_PER_TILE, body, 0)

    def emit(kk):
        d = dils[kk]
        if d == BLOCKWISE:
            return emit_blockwise()
        d_prev = dils[kk - 1] if kk and dils[kk - 1] != BLOCKWISE else 1
        q = d // d_prev
        from_copy = d_prev > 1
        keep_copy = kk + 1 < len(dils) and d > 1
        per_res = BLK_PER_TILE // d

        def body(s, carry):
            r, n = s // per_res, s % per_res
            src = (r % d_prev) * (ROW_TILE // d_prev) + r // d_prev + q * n * WIN
            rows = pl.ds(src, WIN, stride=q) if q > 1 else pl.ds(pl.multiple_of(src, WIN), WIN)
            dst = pl.ds(pl.multiple_of(s * WIN, WIN), WIN)
            rs = (rsp_sc if from_copy else rs_sc)[rows, :]
            if keep_copy:
                rsp_sc[dst, :] = rs
            for c in range(N_CHUNK):
                x = (xp_sc if from_copy else xn_sc)[c, rows, :]
                if keep_copy:
                    xp_sc[c, dst, :] = x
                cols = slice(c * LANES, (c + 1) * LANES)
                o_ref[dst, cols] = ((x * rs) * g_ref[:, cols]).astype(bf16)
            return carry

        lax.fori_loop(0, BLK_PER_TILE, body, 0)

    chain = [d for d in dils if d != BLOCKWISE]
    assert all(d % p == 0 for p, d in zip(chain, chain[1:])) and sum(d > 1 for d in chain[:-1]) <= 1
    for kk in range(len(dils)):
        pl.when(k == kk)(functools.partial(emit, kk))


def _prenorm(x2d, gain, dils):
    m, d_model = x2d.shape
    slabs = pltpu.VMEM((N_CHUNK, ROW_TILE, LANES), f32)
    per_row = pltpu.VMEM((ROW_TILE, LANES), f32)
    return pl.pallas_call(
        functools.partial(_prenorm_kernel, dils=dils),
        grid=(m // ROW_TILE, len(dils)),
        in_specs=[pl.BlockSpec((ROW_TILE, d_model), lambda t, k: (t, 0)),
                  pl.BlockSpec((1, d_model), lambda t, k: (0, 0))],
        out_specs=pl.BlockSpec((None, ROW_TILE, d_model), lambda t, k: (k, t, 0)),
        out_shape=jax.ShapeDtypeStruct((len(dils), m, d_model), bf16),
        scratch_shapes=[slabs, per_row, slabs, per_row],
        compiler_params=_cparams(2),
        name="prenorm",
    )(x2d, gain)


def _proj_kernel(u_ref, w_ref, cos_ref, sin_ref, o_ref, *, rope_secs):
    j = pl.program_id(1)
    n_c = o_ref.shape[0]
    lane = lax.broadcasted_iota(jnp.int32, (1, LANES), 1)
    first_half = (lane % HEAD_DIM) < (HEAD_DIM // 2)
    is_rope = functools.reduce(jnp.logical_or, [j == s for s in rope_secs])

    @pl.when(is_rope)
    def _():
        cos, sin = cos_ref[...], sin_ref[...]
        half = n_c // 2
        for part in range(2):
            cols = slice(part * half * LANES, (part + 1) * half * LANES)
            acc = jnp.dot(u_ref[...], w_ref[:, cols], preferred_element_type=f32)
            for c in range(half):
                x = acc[:, c * LANES:(c + 1) * LANES].astype(bf16)
                partner = jnp.where(first_half,
                                    pltpu.roll(x, LANES - HEAD_DIM // 2, axis=1),
                                    pltpu.roll(x, HEAD_DIM // 2, axis=1))
                o_ref[part * half + c] = x * cos + partner * sin

    @pl.when(jnp.logical_not(is_rope))
    def _():
        acc = jnp.dot(u_ref[...], w_ref[...], preferred_element_type=f32)
        for c in range(n_c):
            o_ref[c] = acc[:, c * LANES:(c + 1) * LANES].astype(bf16)


def _lookup(j, table):
    return sum([jnp.where(j == pos, val, 0) for pos, val in enumerate(table)], jnp.int32(0))


def _in_proj(u, w_bf16, cos_t, sin_t, batch, seq, visit, sec_dsel, q_steps, k_steps, bm=ROW_TILE):
    _, m, d_in = u.shape
    n_out = w_bf16.shape[1]
    bn = D_MODEL
    tiles_per_seq = seq // bm
    sec_tab = [N_ROPE_KINDS * d + (ROPE_Q if j in q_steps else ROPE_K) for j, d in enumerate(sec_dsel)]

    def sec(j):
        return _lookup(j, visit)

    def dsel(j):
        return _lookup(j, sec_dsel)

    def tsel(j):
        return _lookup(j, sec_tab)

    kern = functools.partial(_proj_kernel, rope_secs=sorted(q_steps + k_steps))
    return pl.pallas_call(
        kern,
        grid=(m // bm, n_out // bn),
        in_specs=[
            pl.BlockSpec((None, bm, d_in), lambda i, j: (dsel(j), i, 0)),
            pl.BlockSpec((d_in, bn), lambda i, j: (0, sec(j))),
            pl.BlockSpec((None, bm, LANES), lambda i, j: (tsel(j), i % tiles_per_seq, 0)),
            pl.BlockSpec((None, bm, LANES), lambda i, j: (tsel(j), i % tiles_per_seq, 0)),
        ],
        out_specs=pl.BlockSpec((None, bn // LANES, bm, LANES),
                               lambda i, j: (i // tiles_per_seq, sec(j), i % tiles_per_seq, 0)),
        out_shape=jax.ShapeDtypeStruct((batch, n_out // LANES, seq, LANES), bf16),
        compiler_params=_cparams(2),
        name="in_proj",
    )(u, w_bf16, cos_t, sin_t)


def _dil_attn_kernel(bias_ref, q0, k0, v0, q1, k1, v1, q2, k2, v2, gate_ref, y_ref,
                     num_sc, den_sc, max_sc):
    seq = y_ref.shape[0]
    n_blk = seq // WIN
    lane = lax.broadcasted_iota(jnp.int32, (1, LANES), 1)
    low = lane < HEAD_DIM
    quarter = ROW_TILE // MERGE_ORDER
    outs = (num_sc, den_sc, max_sc)
    sel0 = jnp.where(low, 1.0, 0.0).astype(bf16)
    sel1 = jnp.where(low, 0.0, 1.0).astype(bf16)
    ones = jnp.ones((2 * WIN, LANES), bf16)
    groups = ((q0, k0, v0), (q1, k1, v1), (q2, k2, v2))

    def block(blk):
        t, u = divmod(blk, BLK_PER_TILE)
        row0 = blk * WIN
        for g, (d, order) in enumerate(zip(DILATIONS, ORDERS)):
            assert order == (BLOCKWISE if d == 1 else d)
            q_ref, k_ref, v_ref = groups[g]
            per_res = BLK_PER_TILE // d
            r, n = divmod(u, per_res)
            prev = blk - 1 if n > 0 else blk - BLK_PER_TILE + per_res - 1
            has_prev = prev >= 0
            prow0 = max(prev, 0) * WIN

            q = q_ref[pl.ds(row0, WIN), :]
            qs = jnp.concatenate([q * sel0, q * sel1], axis=0)
            kcat = jnp.concatenate([k_ref[pl.ds(prow0, WIN), :], k_ref[pl.ds(row0, WIN), :]], axis=0)
            vcat = jnp.concatenate([v_ref[pl.ds(prow0, WIN), :], v_ref[pl.ds(row0, WIN), :]], axis=0)
            s = lax.dot_general(qs, kcat, (((1,), (1,)), ((), ())), preferred_element_type=f32)
            s = s + bias_ref[2 * int(order == BLOCKWISE) + int(has_prev)]
            m = jnp.max(s, axis=-1, keepdims=True)
            p = jnp.exp2(s - m).astype(bf16)
            vext = jnp.concatenate([vcat, ones], axis=1)
            pv = jnp.dot(p, vext, preferred_element_type=f32)
            vals = (jnp.where(low, pv[:WIN, :LANES], pv[WIN:, :LANES]),
                    jnp.where(low, pv[:WIN, LANES:], pv[WIN:, LANES:]),
                    jnp.where(low, m[:WIN], m[WIN:]))
            piece = WIN // MERGE_ORDER
            for out, val in zip(outs, vals):
                if order == BLOCKWISE:
                    for b in range(MERGE_ORDER):
                        out[g, pl.ds(t * ROW_TILE + b * quarter + piece * u, piece), :] = (
                            val[b * piece:(b + 1) * piece])
                elif order == MERGE_ORDER:
                    out[g, pl.ds(row0, WIN), :] = val
                else:
                    a, b = divmod(r, MERGE_ORDER)
                    step = d // MERGE_ORDER
                    out[g, pl.ds(t * ROW_TILE + b * quarter + step * n * WIN + a, WIN, stride=step), :] = val

    def merge(i):
        rows = pl.ds(i * WIN, WIN)
        maxs = [max_sc[g, rows, :] for g in range(len(DILATIONS))]
        mx = functools.reduce(jnp.maximum, maxs)
        es = [jnp.exp2(m - mx) for m in maxs]
        num = sum(e * num_sc[g, rows, :] for g, e in enumerate(es))
        den = sum(e * den_sc[g, rows, :] for g, e in enumerate(es))
        gate = gate_ref[rows, :].astype(f32)
        y_ref[rows, :] = ((num * gate) / (den * (1.0 + jnp.exp(-gate)))).astype(bf16)

    for blk in range(n_blk):
        block(blk)
        if blk >= BLK_PER_TILE:
            merge(blk - BLK_PER_TILE)
    for i in range(n_blk - BLK_PER_TILE, n_blk):
        merge(i)


def _dil_attention(proj, batch, seq, sec_of):
    def slab(sec):
        return pl.BlockSpec((None, None, seq, LANES), lambda b, hp: (b, sec * N_CHUNK + hp, 0, 0))

    bias = _window_bias()
    specs = [pl.BlockSpec(bias.shape, lambda b, hp: (0, 0, 0))]
    args = [bias]
    for g in range(len(DILATIONS)):
        for kind in "qkv":
            specs.append(slab(sec_of[kind + str(g)]))
            args.append(proj)
    specs.append(slab(sec_of["gate"]))
    args.append(proj)
    return pl.pallas_call(
        _dil_attn_kernel,
        grid=(batch, N_CHUNK),
        in_specs=specs,
        out_specs=pl.BlockSpec((None, seq, LANES), lambda b, hp: (b, 0, hp)),
        out_shape=jax.ShapeDtypeStruct((batch, seq, D_MODEL), bf16),
        scratch_shapes=[pltpu.VMEM((len(DILATIONS), seq, LANES), f32)] * 3,
        compiler_params=_cparams(2),
        name="dilated_attention",
    )(*args)


TAB_QI, TAB_KJ = range(2)


def _full_tiles(seq, tq, tk):
    pairs = [(qi, kj) for qi in range(seq // tq) for kj in range((qi * tq) // tk)]
    return list(zip(*pairs))


def _diff_attn_kernel(tab_ref, lq1_ref, lk1_ref, lq2_ref, lk2_ref, subln_ref, bias_ref,
                      q_ref, k_ref, v_ref, gate_ref, y_ref, acc_sc, m_sc, *,
                      tq, tk, n_full, unroll, lambda_init):
    lane = lax.broadcasted_iota(jnp.int32, (1, LANES), 1)
    low = lane < HEAD_DIM
    sel0 = jnp.where(low, 1.0, 0.0).astype(bf16)
    sel1 = jnp.where(low, 0.0, 1.0).astype(bf16)
    ones = jnp.ones((tk, LANES), bf16)
    lam = (jnp.exp(jnp.sum(lq1_ref[...] * lk1_ref[...], axis=-1, keepdims=True))
           - jnp.exp(jnp.sum(lq2_ref[...] * lk2_ref[...], axis=-1, keepdims=True)) + lambda_init)

    def scores(q0, k0, width):
        q = q_ref[pl.ds(q0, tq), :]
        qs = jnp.concatenate([q * sel0, q * sel1], axis=0)
        return lax.dot_general(qs, k_ref[pl.ds(k0, width), :], (((1,), (1,)), ((), ())),
                               preferred_element_type=f32)

    def weighted_values(p, k0, width):
        vext = jnp.concatenate([v_ref[pl.ds(k0, width), :], ones[:width]], axis=1)
        return jnp.dot(p, vext, preferred_element_type=f32)

    for qi in range(y_ref.shape[0] // tq):
        k0 = (qi * tq) // tk * tk
        width = (qi + 1) * tq - k0
        bias = bias_ref[(qi * tq - k0) // tq, :, :width]
        s = scores(qi * tq, k0, width) + jnp.tile(bias, (2, 1))
        m = jnp.max(s, axis=-1, keepdims=True)
        m_sc[qi] = jnp.broadcast_to(m, m_sc.shape[1:])
        acc_sc[qi] = weighted_values(jnp.exp2(s - m).astype(bf16), k0, width)

    def step(t):
        qi = tab_ref[TAB_QI, t]
        q0 = pl.multiple_of(qi * tq, tq)
        k0 = pl.multiple_of(tab_ref[TAB_KJ, t] * tk, tk)
        s = scores(q0, k0, tk)
        m_old = m_sc[qi]
        m_new = jnp.maximum(m_old, jnp.max(s, axis=-1, keepdims=True))
        alpha = jnp.exp2(m_old - m_new)
        m_sc[qi] = m_new
        p = jnp.exp2(s - jnp.tile(m_new, (1, tk // LANES))).astype(bf16)
        acc_sc[qi] = jnp.tile(alpha, (1, 2)) * acc_sc[qi] + weighted_values(p, k0, tk)

    def steps(i, carry):
        for k in range(unroll):
            step(unroll * i + k)
        return carry

    assert n_full % unroll == 0
    lax.fori_loop(0, n_full // unroll, steps, 0)

    def finalize(qi, carry):
        q0 = pl.multiple_of(qi * tq, tq)
        acc = acc_sc[qi]
        o = acc[:, :LANES] / acc[:, LANES:]
        o = o[:tq] - lam * o[tq:]
        o = o * lax.rsqrt(jnp.mean(o * o, axis=-1, keepdims=True) + EPS)
        o = o * subln_ref[...] * (1.0 - lambda_init)
        gate = gate_ref[pl.ds(q0, tq), :].astype(f32)
        y_ref[pl.ds(q0, tq), :] = (o * (gate * jax.nn.sigmoid(gate))).astype(bf16)
        return carry

    lax.fori_loop(0, y_ref.shape[0] // tq, finalize, 0, unroll=2)


def _diff_attention(proj, lq1, lk1, lq2, lk2, subln, batch, seq, lambda_init, tq, tk):
    def slab(c0):
        return pl.BlockSpec((None, None, seq, LANES), lambda b, h: (b, c0 + h, 0, 0))

    def small(a):
        return pl.BlockSpec(a.shape, lambda b, h: (0,) * a.ndim)

    table = jnp.array(_full_tiles(seq, tq, tk), jnp.int32)
    n_full = table.shape[1]
    qi = jnp.arange(tq)[:, None]
    kj = jnp.arange(tk)[None, :]
    bias = jnp.stack([jnp.where(kj <= qi + o * tq, 0.0, NEG).astype(f32) for o in range(tk // tq)])
    smalls = [lq1, lk1, lq2, lk2, subln, bias]
    kern = functools.partial(_diff_attn_kernel, tq=tq, tk=tk, n_full=n_full, unroll=28,
                             lambda_init=lambda_init)
    return pl.pallas_call(
        kern,
        grid=(batch, N_CHUNK),
        in_specs=([pl.BlockSpec(memory_space=pltpu.SMEM)] + [small(a) for a in smalls]
                  + [slab(0), slab(N_CHUNK), slab(2 * N_CHUNK), slab(3 * N_CHUNK)]),
        out_specs=pl.BlockSpec((None, seq, LANES), lambda b, h: (b, 0, h)),
        out_shape=jax.ShapeDtypeStruct((batch, seq, D_MODEL), bf16),
        scratch_shapes=[pltpu.VMEM((seq // tq, 2 * tq, 2 * LANES), f32),
                        pltpu.VMEM((seq // tq, 2 * tq, LANES), f32)],
        compiler_params=_cparams(2),
        name="diff_attention",
    )(table, *smalls, proj, proj, proj, proj)


def _residual_update(y_ref, w_ref, g_ref, h_ref):
    z = jnp.dot(y_ref[...], w_ref[...], preferred_element_type=f32)
    z = z * lax.rsqrt(jnp.mean(z * z, axis=-1, keepdims=True) + EPS)
    return h_ref[...] + z * g_ref[...]


def _out_proj_kernel(y_ref, w_ref, g_ref, h_ref, o_ref):
    o_ref[...] = _residual_update(y_ref, w_ref, g_ref, h_ref)


def _out_proj_norm_kernel(y_ref, w_ref, g_ref, h_ref, g_next_ref, o_ref, u_ref):
    h = _residual_update(y_ref, w_ref, g_ref, h_ref)
    o_ref[...] = h
    u_ref[...] = ((h * lax.rsqrt(jnp.mean(h * h, axis=-1, keepdims=True) + EPS)) * g_next_ref[...]).astype(bf16)


def _out_proj_reorder_kernel(y_ref, w_ref, g_ref, h_ref, g_next_ref, o_ref, u_ref, z_sc):
    q, part, d = y_ref.shape
    bm = q * part
    z = jnp.dot(y_ref[...].reshape(bm, d), w_ref[...], preferred_element_type=f32)
    z = (z * lax.rsqrt(jnp.mean(z * z, axis=-1, keepdims=True) + EPS)) * g_ref[...]
    for c in range(d // LANES):
        for b in range(q):
            for j0 in range(0, part, WIN):
                z_sc[c, pl.ds(q * j0 + b, WIN, stride=q), :] = (
                    z[b * part + j0:b * part + j0 + WIN, c * LANES:(c + 1) * LANES])
    sq = jnp.zeros((bm, LANES), f32)
    for c in range(d // LANES):
        cols = slice(c * LANES, (c + 1) * LANES)
        h = h_ref[:, cols] + z_sc[c]
        o_ref[:, cols] = h
        sq = sq + h * h
    rs = lax.rsqrt(jnp.sum(sq, axis=-1, keepdims=True) * (1.0 / d) + EPS)
    for c in range(d // LANES):
        cols = slice(c * LANES, (c + 1) * LANES)
        u_ref[:, cols] = ((o_ref[:, cols] * rs) * g_next_ref[:, cols]).astype(bf16)


def _out_proj(y2d, w_bf16, gain, h2d, next_gain=None, y_order=1, bm=1024):
    m, d = h2d.shape
    row = pl.BlockSpec((bm, d), lambda i: (i, 0))
    vec = pl.BlockSpec((1, d), lambda i: (0, 0))
    in_specs = [row, pl.BlockSpec((d, d), lambda i: (0, 0)), vec, row]
    if y_order > 1:
        q, steps_per_tile = y_order, ROW_TILE // bm
        part = bm // q
        y5d = y2d.reshape(m // ROW_TILE, q, steps_per_tile, part, d)
        in_specs[0] = pl.BlockSpec((None, q, None, part, d),
                                   lambda i: (i // steps_per_tile, 0, i % steps_per_tile, 0, 0))
        return pl.pallas_call(
            _out_proj_reorder_kernel, grid=(m // bm,), in_specs=in_specs + [vec],
            out_specs=[row, pl.BlockSpec((None, bm, d), lambda i: (0, i, 0))],
            out_shape=[jax.ShapeDtypeStruct((m, d), f32), jax.ShapeDtypeStruct((1, m, d), bf16)],
            scratch_shapes=[pltpu.VMEM((d // LANES, bm, LANES), f32)],
            compiler_params=_cparams(1), name="out_proj_reorder",
        )(y5d, w_bf16, gain, h2d, next_gain)
    if next_gain is None:
        return pl.pallas_call(
            _out_proj_kernel, grid=(m // bm,), in_specs=in_specs, out_specs=row,
            out_shape=jax.ShapeDtypeStruct((m, d), f32),
            compiler_params=_cparams(1), name="out_proj",
        )(y2d, w_bf16, gain, h2d)
    return pl.pallas_call(
        _out_proj_norm_kernel, grid=(m // bm,), in_specs=in_specs + [vec],
        out_specs=[row, pl.BlockSpec((None, bm, d), lambda i: (0, i, 0))],
        out_shape=[jax.ShapeDtypeStruct((m, d), f32), jax.ShapeDtypeStruct((1, m, d), bf16)],
        compiler_params=_cparams(1), name="out_proj_norm",
    )(y2d, w_bf16, gain, h2d, next_gain)


def _rope_tables(seq, dils):
    half = HEAD_DIM // 2
    freqs = ROPE_THETA ** (-jnp.arange(0, HEAD_DIM, 2, dtype=f32) / HEAD_DIM)
    lane = jnp.arange(LANES)
    freq_l = freqs[lane % half][None, None, :]
    sign_l = jnp.where(lane % HEAD_DIM < half, -1.0, 1.0).astype(f32)[None, None, :]
    row = jnp.arange(seq)

    def position(order):
        tile, d = (WIN, 4) if order == BLOCKWISE else (ROW_TILE, order)
        w = row % tile
        return row - w + (w % (tile // d)) * d + w // (tile // d)

    pos = jnp.stack([position(order) for order in dils])
    ang = pos.astype(f32)[:, None, :, None] * freq_l
    kind_scale = jnp.array([1.0, Q_SCALE], f32)[None, :, None, None]
    cos = (jnp.cos(ang) * kind_scale).astype(bf16)
    sin = (sign_l * jnp.sin(ang) * kind_scale).astype(bf16)
    return cos.reshape(-1, seq, LANES), sin.reshape(-1, seq, LANES)


def _window_bias():
    def masks(step_of_row):
        qi = step_of_row[jnp.arange(2 * WIN) % WIN][:, None]
        kj = jnp.arange(2 * WIN)[None, :]
        kstep = step_of_row[kj % WIN]
        cur = (kj >= WIN) & (kstep <= qi)
        prev = (kj < WIN) & (kstep >= qi)
        return [jnp.where(cur, 0.0, NEG), jnp.where(cur | prev, 0.0, NEG)]

    rows = jnp.arange(WIN)
    piece = WIN // 4
    return jnp.stack(masks(rows) + masks((rows % piece) * 4 + rows // piece)).astype(f32)


def kernel(x, norm_pre, norm_post, dil_w_in, dil_w_out, diff_w_in, diff_w_out, diff_lambda_q1, diff_lambda_k1, diff_lambda_q2, diff_lambda_k2, diff_subln):
    batch, seq, d_model = x.shape
    assert d_model == D_MODEL and seq % ROW_TILE == 0
    n_g = len(DILATIONS)
    h = x.reshape(batch * seq, d_model)

    names = [kind + str(g) for kind in "qkv" for g in range(n_g)] + ["gate"]
    order_idx = [ORDERS.index(MERGE_ORDER) if name == "gate" else int(name[1]) for name in names]
    visit = sorted(range(len(names)), key=lambda s: (order_idx[s], s))
    sec_of = {name: s for s, name in enumerate(names)}
    sec_dsel = [order_idx[s] for s in visit]
    q_steps = [j for j, s in enumerate(visit) if names[s][0] == "q"]
    k_steps = [j for j, s in enumerate(visit) if names[s][0] == "k"]
    cos_t, sin_t = _rope_tables(seq, ORDERS)
    u = _prenorm(h, norm_pre[0][None, :], ORDERS)
    proj = _in_proj(u, dil_w_in[0].astype(bf16), cos_t, sin_t, batch, seq, visit, sec_dsel, q_steps, k_steps)
    y = _dil_attention(proj, batch, seq, sec_of)
    h, u = _out_proj(y.reshape(batch * seq, d_model), dil_w_out[0].astype(bf16), norm_post[0][None, :], h,
                     next_gain=norm_pre[1][None, :], y_order=MERGE_ORDER)

    lambda_init = 0.8 - 0.6 * math.exp(-0.3 * 1)
    tq, tk = 256, 512
    cos_t, sin_t = _rope_tables(seq, (1,))
    proj = _in_proj(u, diff_w_in[0].astype(bf16), cos_t, sin_t, batch, seq, [0, 1, 2, 3], [0, 0, 0, 0], [0], [1])
    y = _diff_attention(proj, diff_lambda_q1[0][None, :], diff_lambda_k1[0][None, :],
                        diff_lambda_q2[0][None, :], diff_lambda_k2[0][None, :],
                        diff_subln[0][None, :], batch, seq, lambda_init, tq, tk)
    h = _out_proj(y.reshape(batch * seq, d_model), diff_w_out[0].astype(bf16), norm_post[1][None, :], h)
    return h.reshape(batch, seq, d_model)
```

```python
import functools
import math

import jax
import jax.numpy as jnp
from jax import lax
from jax.experimental import pallas as pl
from jax.experimental.pallas import tpu as pltpu

D_MODEL = 1024
HEAD_DIM = 64
ROPE_THETA = 10000.0
EPS = 1e-6
DILATIONS = (1, 4, 16)
WIN = 128
LANES = 128
N_CHUNK = D_MODEL // LANES
ROW_TILE = 2048
BLK_PER_TILE = ROW_TILE // WIN
NEG = -1e30
Q_SCALE = HEAD_DIM ** -0.5 * math.log2(math.e)
BLOCKWISE = 0
ORDERS = (BLOCKWISE, 4, 16)
MERGE_ORDER = 4
ROPE_K, ROPE_Q, N_ROPE_KINDS = 0, 1, 2
VMEM_LIMIT = 52 * 1024 * 1024

f32 = jnp.float32
bf16 = jnp.bfloat16


def _cparams(n_axes):
    return pltpu.CompilerParams(
        dimension_semantics=("arbitrary",) * n_axes, vmem_limit_bytes=VMEM_LIMIT)


def _prenorm_kernel(x_ref, g_ref, o_ref, xn_sc, rs_sc, xp_sc, rsp_sc, *, dils):
    k = pl.program_id(1)
    d_model = N_CHUNK * LANES
    sub = 2 * WIN

    @pl.when(k == 0)
    def _():
        def scale(i, carry):
            rows = pl.ds(pl.multiple_of(i * sub, sub), sub)
            sq = jnp.zeros((sub, LANES), f32)
            for c in range(N_CHUNK):
                x = x_ref[rows, c * LANES:(c + 1) * LANES]
                xn_sc[c, rows, :] = x
                sq = sq + x * x
            ss = jnp.sum(sq, axis=-1, keepdims=True)
            rs_sc[rows, :] = jnp.broadcast_to(lax.rsqrt(ss * (1.0 / d_model) + EPS), (sub, LANES))
            return carry

        lax.fori_loop(0, ROW_TILE // sub, scale, 0)

    def emit_blockwise():
        def body(s, carry):
            for r in range(4):
                rows = pl.ds(s * WIN + r, WIN // 4, stride=4)
                dst = pl.ds(pl.multiple_of(s * WIN + r * (WIN // 4), WIN // 4), WIN // 4)
                rs = rs_sc[rows, :]
                for c in range(N_CHUNK):
                    cols = slice(c * LANES, (c + 1) * LANES)
                    o_ref[dst, cols] = ((xn_sc[c, rows, :] * rs) * g_ref[:, cols]).astype(bf16)
            return carry

        lax.fori_loop(0, BLK_PER_TILE, body, 0)

    def emit(kk):
        d = dils[kk]
        if d == BLOCKWISE:
            return emit_blockwise()
        d_prev = dils[kk - 1] if kk and dils[kk - 1] != BLOCKWISE else 1
        q = d // d_prev
        from_copy = d_prev > 1
        keep_copy = kk + 1 < len(dils) and d > 1
        per_res = BLK_PER_TILE // d

        def body(s, carry):
            r, n = s // per_res, s % per_res
            src = (r % d_prev) * (ROW_TILE // d_prev) + r // d_prev + q * n * WIN
            rows = pl.ds(src, WIN, stride=q) if q > 1 else pl.ds(pl.multiple_of(src, WIN), WIN)
            dst = pl.ds(pl.multiple_of(s * WIN, WIN), WIN)
            rs = (rsp_sc if from_copy else rs_sc)[rows, :]
            if keep_copy:
                rsp_sc[dst, :] = rs
            for c in range(N_CHUNK):
                x = (xp_sc if from_copy else xn_sc)[c, rows, :]
                if keep_copy:
                    xp_sc[c, dst, :] = x
                cols = slice(c * LANES, (c + 1) * LANES)
                o_ref[dst, cols] = ((x * rs) * g_ref[:, cols]).astype(bf16)
            return carry

        lax.fori_loop(0, BLK_PER_TILE, body, 0)

    chain = [d for d in dils if d != BLOCKWISE]
    assert all(d % p == 0 for p, d in zip(chain, chain[1:])) and sum(d > 1 for d in chain[:-1]) <= 1
    for kk in range(len(dils)):
        pl.when(k == kk)(functools.partial(emit, kk))


def _prenorm(x2d, gain, dils):
    m, d_model = x2d.shape
    slabs = pltpu.VMEM((N_CHUNK, ROW_TILE, LANES), f32)
    per_row = pltpu.VMEM((ROW_TILE, LANES), f32)
    return pl.pallas_call(
        functools.partial(_prenorm_kernel, dils=dils),
        grid=(m // ROW_TILE, len(dils)),
        in_specs=[pl.BlockSpec((ROW_TILE, d_model), lambda t, k: (t, 0)),
                  pl.BlockSpec((1, d_model), lambda t, k: (0, 0))],
        out_specs=pl.BlockSpec((None, ROW_TILE, d_model), lambda t, k: (k, t, 0)),
        out_shape=jax.ShapeDtypeStruct((len(dils), m, d_model), bf16),
        scratch_shapes=[slabs, per_row, slabs, per_row],
        compiler_params=_cparams(2),
        name="prenorm",
    )(x2d, gain)


def _proj_kernel(u_ref, w_ref, cos_ref, sin_ref, o_ref, *, rope_secs):
    j = pl.program_id(1)
    n_c = o_ref.shape[0]
    lane = lax.broadcasted_iota(jnp.int32, (1, LANES), 1)
    first_half = (lane % HEAD_DIM) < (HEAD_DIM // 2)
    is_rope = functools.reduce(jnp.logical_or, [j == s for s in rope_secs])

    @pl.when(is_rope)
    def _():
        cos, sin = cos_ref[...], sin_ref[...]
        half = n_c // 2
        for part in range(2):
            cols = slice(part * half * LANES, (part + 1) * half * LANES)
            acc = jnp.dot(u_ref[...], w_ref[:, cols], preferred_element_type=f32)
            for c in range(half):
                x = acc[:, c * LANES:(c + 1) * LANES].astype(bf16)
                partner = jnp.where(first_half,
                                    pltpu.roll(x, LANES - HEAD_DIM // 2, axis=1),
                                    pltpu.roll(x, HEAD_DIM // 2, axis=1))
                o_ref[part * half + c] = x * cos + partner * sin

    @pl.when(jnp.logical_not(is_rope))
    def _():
        acc = jnp.dot(u_ref[...], w_ref[...], preferred_element_type=f32)
        for c in range(n_c):
            o_ref[c] = acc[:, c * LANES:(c + 1) * LANES].astype(bf16)


def _lookup(j, table):
    return sum([jnp.where(j == pos, val, 0) for pos, val in enumerate(table)], jnp.int32(0))


def _in_proj(u, w_bf16, cos_t, sin_t, batch, seq, visit, sec_dsel, q_steps, k_steps, bm=ROW_TILE):
    _, m, d_in = u.shape
    n_out = w_bf16.shape[1]
    bn = D_MODEL
    tiles_per_seq = seq // bm
    sec_tab = [N_ROPE_KINDS * d + (ROPE_Q if j in q_steps else ROPE_K) for j, d in enumerate(sec_dsel)]

    def sec(j):
        return _lookup(j, visit)

    def dsel(j):
        return _lookup(j, sec_dsel)

    def tsel(j):
        return _lookup(j, sec_tab)

    kern = functools.partial(_proj_kernel, rope_secs=sorted(q_steps + k_steps))
    return pl.pallas_call(
        kern,
        grid=(m // bm, n_out // bn),
        in_specs=[
            pl.BlockSpec((None, bm, d_in), lambda i, j: (dsel(j), i, 0)),
            pl.BlockSpec((d_in, bn), lambda i, j: (0, sec(j))),
            pl.BlockSpec((None, bm, LANES), lambda i, j: (tsel(j), i % tiles_per_seq, 0)),
            pl.BlockSpec((None, bm, LANES), lambda i, j: (tsel(j), i % tiles_per_seq, 0)),
        ],
        out_specs=pl.BlockSpec((None, bn // LANES, bm, LANES),
                               lambda i, j: (i // tiles_per_seq, sec(j), i % tiles_per_seq, 0)),
        out_shape=jax.ShapeDtypeStruct((batch, n_out // LANES, seq, LANES), bf16),
        compiler_params=_cparams(2),
        name="in_proj",
    )(u, w_bf16, cos_t, sin_t)


def _dil_attn_kernel(bias_ref, q0, k0, v0, q1, k1, v1, q2, k2, v2, gate_ref, y_ref,
                     num_sc, den_sc, max_sc):
    seq = y_ref.shape[0]
    n_blk = seq // WIN
    lane = lax.broadcasted_iota(jnp.int32, (1, LANES), 1)
    low = lane < HEAD_DIM
    quarter = ROW_TILE // MERGE_ORDER
    outs = (num_sc, den_sc, max_sc)
    sel0 = jnp.where(low, 1.0, 0.0).astype(bf16)
    sel1 = jnp.where(low, 0.0, 1.0).astype(bf16)
    ones = jnp.ones((2 * WIN, LANES), bf16)
    groups = ((q0, k0, v0), (q1, k1, v1), (q2, k2, v2))

    def block(blk):
        t, u = divmod(blk, BLK_PER_TILE)
        row0 = blk * WIN
        for g, (d, order) in enumerate(zip(DILATIONS, ORDERS)):
            assert order == (BLOCKWISE if d == 1 else d)
            q_ref, k_ref, v_ref = groups[g]
            per_res = BLK_PER_TILE // d
            r, n = divmod(u, per_res)
            prev = blk - 1 if n > 0 else blk - BLK_PER_TILE + per_res - 1
            has_prev = prev >= 0
            prow0 = max(prev, 0) * WIN

            q = q_ref[pl.ds(row0, WIN), :]
            qs = jnp.concatenate([q * sel0, q * sel1], axis=0)
            kcat = jnp.concatenate([k_ref[pl.ds(prow0, WIN), :], k_ref[pl.ds(row0, WIN), :]], axis=0)
            vcat = jnp.concatenate([v_ref[pl.ds(prow0, WIN), :], v_ref[pl.ds(row0, WIN), :]], axis=0)
            s = lax.dot_general(qs, kcat, (((1,), (1,)), ((), ())), preferred_element_type=f32)
            s = s + bias_ref[2 * int(order == BLOCKWISE) + int(has_prev)]
            m = jnp.max(s, axis=-1, keepdims=True)
            p = jnp.exp2(s - m).astype(bf16)
            vext = jnp.concatenate([vcat, ones], axis=1)
            pv = jnp.dot(p, vext, preferred_element_type=f32)
            vals = (jnp.where(low, pv[:WIN, :LANES], pv[WIN:, :LANES]),
                    jnp.where(low, pv[:WIN, LANES:], pv[WIN:, LANES:]),
                    jnp.where(low, m[:WIN], m[WIN:]))
            piece = WIN // MERGE_ORDER
            for out, val in zip(outs, vals):
                if order == BLOCKWISE:
                    for b in range(MERGE_ORDER):
                        out[g, pl.ds(t * ROW_TILE + b * quarter + piece * u, piece), :] = (
                            val[b * piece:(b + 1) * piece])
                elif order == MERGE_ORDER:
                    out[g, pl.ds(row0, WIN), :] = val
                else:
                    a, b = divmod(r, MERGE_ORDER)
                    step = d // MERGE_ORDER
                    out[g, pl.ds(t * ROW_TILE + b * quarter + step * n * WIN + a, WIN, stride=step), :] = val

    def merge(i):
        rows = pl.ds(i * WIN, WIN)
        maxs = [max_sc[g, rows, :] for g in range(len(DILATIONS))]
        mx = functools.reduce(jnp.maximum, maxs)
        es = [jnp.exp2(m - mx) for m in maxs]
        num = sum(e * num_sc[g, rows, :] for g, e in enumerate(es))
        den = sum(e * den_sc[g, rows, :] for g, e in enumerate(es))
        gate = gate_ref[rows, :].astype(f32)
        y_ref[rows, :] = ((num * gate) / (den * (1.0 + jnp.exp(-gate)))).astype(bf16)

    for blk in range(n_blk):
        block(blk)
        if blk >= BLK_PER_TILE:
            merge(blk - BLK_PER_TILE)
    for i in range(n_blk - BLK_PER_TILE, n_blk):
        merge(i)


def _dil_attention(proj, batch, seq, sec_of):
    def slab(sec):
        return pl.BlockSpec((None, None, seq, LANES), lambda b, hp: (b, sec * N_CHUNK + hp, 0, 0))

    bias = _window_bias()
    specs = [pl.BlockSpec(bias.shape, lambda b, hp: (0, 0, 0))]
    args = [bias]
    for g in range(len(DILATIONS)):
        for kind in "qkv":
            specs.append(slab(sec_of[kind + str(g)]))
            args.append(proj)
    specs.append(slab(sec_of["gate"]))
    args.append(proj)
    return pl.pallas_call(
        _dil_attn_kernel,
        grid=(batch, N_CHUNK),
        in_specs=specs,
        out_specs=pl.BlockSpec((None, seq, LANES), lambda b, hp: (b, 0, hp)),
        out_shape=jax.ShapeDtypeStruct((batch, seq, D_MODEL), bf16),
        scratch_shapes=[pltpu.VMEM((len(DILATIONS), seq, LANES), f32)] * 3,
        compiler_params=_cparams(2),
        name="dilated_attention",
    )(*args)


TAB_QI, TAB_KJ = range(2)


def _full_tiles(seq, tq, tk):
    pairs = [(qi, kj) for qi in range(seq // tq) for kj in range((qi * tq) // tk)]
    return list(zip(*pairs))


def _diff_attn_kernel(tab_ref, lq1_ref, lk1_ref, lq2_ref, lk2_ref, subln_ref, bias_ref,
                      q_ref, k_ref, v_ref, gate_ref, y_ref, acc_sc, m_sc, *,
                      tq, tk, n_full, unroll, lambda_init):
    lane = lax.broadcasted_iota(jnp.int32, (1, LANES), 1)
    low = lane < HEAD_DIM
    sel0 = jnp.where(low, 1.0, 0.0).astype(bf16)
    sel1 = jnp.where(low, 0.0, 1.0).astype(bf16)
    ones = jnp.ones((tk, LANES), bf16)
    lam = (jnp.exp(jnp.sum(lq1_ref[...] * lk1_ref[...], axis=-1, keepdims=True))
           - jnp.exp(jnp.sum(lq2_ref[...] * lk2_ref[...], axis=-1, keepdims=True)) + lambda_init)

    def scores(q0, k0, width):
        q = q_ref[pl.ds(q0, tq), :]
        qs = jnp.concatenate([q * sel0, q * sel1], axis=0)
        return lax.dot_general(qs, k_ref[pl.ds(k0, width), :], (((1,), (1,)), ((), ())),
                               preferred_element_type=f32)

    def weighted_values(p, k0, width):
        vext = jnp.concatenate([v_ref[pl.ds(k0, width), :], ones[:width]], axis=1)
        return jnp.dot(p, vext, preferred_element_type=f32)

    for qi in range(y_ref.shape[0] // tq):
        k0 = (qi * tq) // tk * tk
        width = (qi + 1) * tq - k0
        bias = bias_ref[(qi * tq - k0) // tq, :, :width]
        s = scores(qi * tq, k0, width) + jnp.tile(bias, (2, 1))
        m = jnp.max(s, axis=-1, keepdims=True)
        m_sc[qi] = jnp.broadcast_to(m, m_sc.shape[1:])
        acc_sc[qi] = weighted_values(jnp.exp2(s - m).astype(bf16), k0, width)

    def step(t):
        qi = tab_ref[TAB_QI, t]
        q0 = pl.multiple_of(qi * tq, tq)
        k0 = pl.multiple_of(tab_ref[TAB_KJ, t] * tk, tk)
        s = scores(q0, k0, tk)
        m_old = m_sc[qi]
        m_new = jnp.maximum(m_old, jnp.max(s, axis=-1, keepdims=True))
        alpha = jnp.exp2(m_old - m_new)
        m_sc[qi] = m_new
        p = jnp.exp2(s - jnp.tile(m_new, (1, tk // LANES))).astype(bf16)
        acc_sc[qi] = jnp.tile(alpha, (1, 2)) * acc_sc[qi] + weighted_values(p, k0, tk)

    def steps(i, carry):
        for k in range(unroll):
            step(unroll * i + k)
        return carry

    assert n_full % unroll == 0
    lax.fori_loop(0, n_full // unroll, steps, 0)

    def finalize(qi, carry):
        q0 = pl.multiple_of(qi * tq, tq)
        acc = acc_sc[qi]
        o = acc[:, :LANES] / acc[:, LANES:]
        o = o[:tq] - lam * o[tq:]
        o = o * lax.rsqrt(jnp.mean(o * o, axis=-1, keepdims=True) + EPS)
        o = o * subln_ref[...] * (1.0 - lambda_init)
        gate = gate_ref[pl.ds(q0, tq), :].astype(f32)
        y_ref[pl.ds(q0, tq), :] = (o * (gate * jax.nn.sigmoid(gate))).astype(bf16)
        return carry

    lax.fori_loop(0, y_ref.shape[0] // tq, finalize, 0, unroll=2)


def _diff_attention(proj, lq1, lk1, lq2, lk2, subln, batch, seq, lambda_init, tq, tk):
    def slab(c0):
        return pl.BlockSpec((None, None, seq, LANES), lambda b, h: (b, c0 + h, 0, 0))

    def small(a):
        return pl.BlockSpec(a.shape, lambda b, h: (0,) * a.ndim)

    table = jnp.array(_full_tiles(seq, tq, tk), jnp.int32)
    n_full = table.shape[1]
    qi = jnp.arange(tq)[:, None]
    kj = jnp.arange(tk)[None, :]
    bias = jnp.stack([jnp.where(kj <= qi + o * tq, 0.0, NEG).astype(f32) for o in range(tk // tq)])
    smalls = [lq1, lk1, lq2, lk2, subln, bias]
    kern = functools.partial(_diff_attn_kernel, tq=tq, tk=tk, n_full=n_full, unroll=28,
                             lambda_init=lambda_init)
    return pl.pallas_call(
        kern,
        grid=(batch, N_CHUNK),
        in_specs=([pl.BlockSpec(memory_space=pltpu.SMEM)] + [small(a) for a in smalls]
                  + [slab(0), slab(N_CHUNK), slab(2 * N_CHUNK), slab(3 * N_CHUNK)]),
        out_specs=pl.BlockSpec((None, seq, LANES), lambda b, h: (b, 0, h)),
        out_shape=jax.ShapeDtypeStruct((batch, seq, D_MODEL), bf16),
        scratch_shapes=[pltpu.VMEM((seq // tq, 2 * tq, 2 * LANES), f32),
                        pltpu.VMEM((seq // tq, 2 * tq, LANES), f32)],
        compiler_params=_cparams(2),
        name="diff_attention",
    )(table, *smalls, proj, proj, proj, proj)


def _out_proj_kernel(y_ref, w_ref, g_ref, h_ref, o_ref):
    z = jnp.dot(y_ref[...], w_ref[...], preferred_element_type=f32)
    z = z * lax.rsqrt(jnp.mean(z * z, axis=-1, keepdims=True) + EPS)
    o_ref[...] = h_ref[...] + z * g_ref[...]


def _out_proj_reorder_kernel(y_ref, w_ref, g_ref, h_ref, g_next_ref, o_ref, u_ref, z_sc):
    q, part, d = y_ref.shape
    bm = q * part
    z = jnp.dot(y_ref[...].reshape(bm, d), w_ref[...], preferred_element_type=f32)
    z = (z * lax.rsqrt(jnp.mean(z * z, axis=-1, keepdims=True) + EPS)) * g_ref[...]
    for c in range(d // LANES):
        for b in range(q):
            for j0 in range(0, part, WIN):
                z_sc[c, pl.ds(q * j0 + b, WIN, stride=q), :] = (
                    z[b * part + j0:b * part + j0 + WIN, c * LANES:(c + 1) * LANES])
    sq = jnp.zeros((bm, LANES), f32)
    for c in range(d // LANES):
        cols = slice(c * LANES, (c + 1) * LANES)
        h = h_ref[:, cols] + z_sc[c]
        o_ref[:, cols] = h
        sq = sq + h * h
    rs = lax.rsqrt(jnp.sum(sq, axis=-1, keepdims=True) * (1.0 / d) + EPS)
    for c in range(d // LANES):
        cols = slice(c * LANES, (c + 1) * LANES)
        u_ref[:, cols] = ((o_ref[:, cols] * rs) * g_next_ref[:, cols]).astype(bf16)


def _out_proj(y2d, w_bf16, gain, h2d, next_gain=None, y_order=1, bm=1024):
    m, d = h2d.shape
    row = pl.BlockSpec((bm, d), lambda i: (i, 0))
    vec = pl.BlockSpec((1, d), lambda i: (0, 0))
    in_specs = [row, pl.BlockSpec((d, d), lambda i: (0, 0)), vec, row]
    assert (y_order > 1) == (next_gain is not None)
    if y_order > 1:
        q, steps_per_tile = y_order, ROW_TILE // bm
        part = bm // q
        y5d = y2d.reshape(m // ROW_TILE, q, steps_per_tile, part, d)
        in_specs[0] = pl.BlockSpec((None, q, None, part, d),
                                   lambda i: (i // steps_per_tile, 0, i % steps_per_tile, 0, 0))
        return pl.pallas_call(
            _out_proj_reorder_kernel, grid=(m // bm,), in_specs=in_specs + [vec],
            out_specs=[row, pl.BlockSpec((None, bm, d), lambda i: (0, i, 0))],
            out_shape=[jax.ShapeDtypeStruct((m, d), f32), jax.ShapeDtypeStruct((1, m, d), bf16)],
            scratch_shapes=[pltpu.VMEM((d // LANES, bm, LANES), f32)],
            compiler_params=_cparams(1), name="out_proj_reorder",
        )(y5d, w_bf16, gain, h2d, next_gain)
    return pl.pallas_call(
        _out_proj_kernel, grid=(m // bm,), in_specs=in_specs, out_specs=row,
        out_shape=jax.ShapeDtypeStruct((m, d), f32),
        compiler_params=_cparams(1), name="out_proj",
    )(y2d, w_bf16, gain, h2d)


def _rope_tables(seq, dils):
    half = HEAD_DIM // 2
    freqs = ROPE_THETA ** (-jnp.arange(0, HEAD_DIM, 2, dtype=f32) / HEAD_DIM)
    lane = jnp.arange(LANES)
    freq_l = freqs[lane % half][None, None, :]
    sign_l = jnp.where(lane % HEAD_DIM < half, -1.0, 1.0).astype(f32)[None, None, :]
    row = jnp.arange(seq)

    def position(order):
        tile, d = (WIN, 4) if order == BLOCKWISE else (ROW_TILE, order)
        w = row % tile
        return row - w + (w % (tile // d)) * d + w // (tile // d)

    pos = jnp.stack([position(order) for order in dils])
    ang = pos.astype(f32)[:, None, :, None] * freq_l
    kind_scale = jnp.array([1.0, Q_SCALE], f32)[None, :, None, None]
    cos = (jnp.cos(ang) * kind_scale).astype(bf16)
    sin = (sign_l * jnp.sin(ang) * kind_scale).astype(bf16)
    return cos.reshape(-1, seq, LANES), sin.reshape(-1, seq, LANES)


def _window_bias():
    def masks(step_of_row):
        qi = step_of_row[jnp.arange(2 * WIN) % WIN][:, None]
        kj = jnp.arange(2 * WIN)[None, :]
        kstep = step_of_row[kj % WIN]
        cur = (kj >= WIN) & (kstep <= qi)
        prev = (kj < WIN) & (kstep >= qi)
        return [jnp.where(cur, 0.0, NEG), jnp.where(cur | prev, 0.0, NEG)]

    rows = jnp.arange(WIN)
    piece = WIN // 4
    return jnp.stack(masks(rows) + masks((rows % piece) * 4 + rows // piece)).astype(f32)


def kernel(x, norm_pre, norm_post, dil_w_in, dil_w_out, diff_w_in, diff_w_out, diff_lambda_q1, diff_lambda_k1, diff_lambda_q2, diff_lambda_k2, diff_subln):
    batch, seq, d_model = x.shape
    assert d_model == D_MODEL and seq % ROW_TILE == 0
    n_g = len(DILATIONS)
    h = x.reshape(batch * seq, d_model)

    names = [kind + str(g) for kind in "qkv" for g in range(n_g)] + ["gate"]
    order_idx = [ORDERS.index(MERGE_ORDER) if name == "gate" else int(name[1]) for name in names]
    visit = sorted(range(len(names)), key=lambda s: (order_idx[s], s))
    sec_of = {name: s for s, name in enumerate(names)}
    sec_dsel = [order_idx[s] for s in visit]
    q_steps = [j for j, s in enumerate(visit) if names[s][0] == "q"]
    k_steps = [j for j, s in enumerate(visit) if names[s][0] == "k"]
    cos_t, sin_t = _rope_tables(seq, ORDERS)
    u = _prenorm(h, norm_pre[0][None, :], ORDERS)
    proj = _in_proj(u, dil_w_in[0].astype(bf16), cos_t, sin_t, batch, seq, visit, sec_dsel, q_steps, k_steps)
    y = _dil_attention(proj, batch, seq, sec_of)
    h, u = _out_proj(y.reshape(batch * seq, d_model), dil_w_out[0].astype(bf16), norm_post[0][None, :], h,
                     next_gain=norm_pre[1][None, :], y_order=MERGE_ORDER)

    lambda_init = 0.8 - 0.6 * math.exp(-0.3 * 1)
    tq, tk = 256, 512
    cos_t, sin_t = _rope_tables(seq, (1,))
    proj = _in_proj(u, diff_w_in[0].astype(bf16), cos_t, sin_t, batch, seq, [0, 1, 2, 3], [0, 0, 0, 0], [0], [1])
    y = _diff_attention(proj, diff_lambda_q1[0][None, :], diff_lambda_k1[0][None, :],
                        diff_lambda_q2[0][None, :], diff_lambda_k2[0][None, :],
                        diff_subln[0][None, :], batch, seq, lambda_init, tq, tk)
    h = _out_proj(y.reshape(batch * seq, d_model), diff_w_out[0].astype(bf16), norm_post[1][None, :], h)
    return h.reshape(batch, seq, d_model)
```

```python
import functools
import math

import jax
import jax.numpy as jnp
from jax import lax
from jax.experimental import pallas as pl
from jax.experimental.pallas import tpu as pltpu

D_MODEL = 1024
HEAD_DIM = 64
ROPE_THETA = 10000.0
EPS = 1e-6
DILATIONS = (1, 4, 16)
WIN = 128
LANES = 128
N_CHUNK = D_MODEL // LANES
ROW_TILE = 2048
BLK_PER_TILE = ROW_TILE // WIN
NEG = -1e30
Q_SCALE = HEAD_DIM ** -0.5 * math.log2(math.e)
BLOCKWISE = 0
ORDERS = (BLOCKWISE, 4, 16)
MERGE_ORDER = 4
ROPE_K, ROPE_Q, N_ROPE_KINDS = 0, 1, 2
VMEM_LIMIT = 52 * 1024 * 1024

f32 = jnp.float32
bf16 = jnp.bfloat16


def _cparams(n_axes):
    return pltpu.CompilerParams(
        dimension_semantics=("arbitrary",) * n_axes, vmem_limit_bytes=VMEM_LIMIT)


def _prenorm_kernel(x_ref, g_ref, o_ref, xn_sc, rs_sc, xp_sc, rsp_sc, *, dils):
    k = pl.program_id(1)
    d_model = N_CHUNK * LANES
    sub = 2 * WIN

    @pl.when(k == 0)
    def _():
        def scale(i, carry):
            rows = pl.ds(pl.multiple_of(i * sub, sub), sub)
            sq = jnp.zeros((sub, LANES), f32)
            for c in range(N_CHUNK):
                x = x_ref[rows, c * LANES:(c + 1) * LANES]
                xn_sc[c, rows, :] = x
                sq = sq + x * x
            ss = jnp.sum(sq, axis=-1, keepdims=True)
            rs_sc[rows, :] = jnp.broadcast_to(lax.rsqrt(ss * (1.0 / d_model) + EPS), (sub, LANES))
            return carry

        lax.fori_loop(0, ROW_TILE // sub, scale, 0)

    def emit_blockwise():
        def body(s, carry):
            for r in range(4):
                rows = pl.ds(s * WIN + r, WIN // 4, stride=4)
                dst = pl.ds(pl.multiple_of(s * WIN + r * (WIN // 4), WIN // 4), WIN // 4)
                rs = rs_sc[rows, :]
                for c in range(N_CHUNK):
                    cols = slice(c * LANES, (c + 1) * LANES)
                    o_ref[dst, cols] = ((xn_sc[c, rows, :] * rs) * g_ref[:, cols]).astype(bf16)
            return carry

        lax.fori_loop(0, BLK_PER_TILE, body, 0)

    def emit(kk):
        d = dils[kk]
        if d == BLOCKWISE:
            return emit_blockwise()
        d_prev = dils[kk - 1] if kk and dils[kk - 1] != BLOCKWISE else 1
        q = d // d_prev
        from_copy = d_prev > 1
        keep_copy = kk + 1 < len(dils) and d > 1
        per_res = BLK_PER_TILE // d

        def body(s, carry):
            r, n = s // per_res, s % per_res
            src = (r % d_prev) * (ROW_TILE // d_prev) + r // d_prev + q * n * WIN
            rows = pl.ds(src, WIN, stride=q) if q > 1 else pl.ds(pl.multiple_of(src, WIN), WIN)
            dst = pl.ds(pl.multiple_of(s * WIN, WIN), WIN)
            rs = (rsp_sc if from_copy else rs_sc)[rows, :]
            if keep_copy:
                rsp_sc[dst, :] = rs
            for c in range(N_CHUNK):
                x = (xp_sc if from_copy else xn_sc)[c, rows, :]
                if keep_copy:
                    xp_sc[c, dst, :] = x
                cols = slice(c * LANES, (c + 1) * LANES)
                o_ref[dst, cols] = ((x * rs) * g_ref[:, cols]).astype(bf16)
            return carry

        lax.fori_loop(0, BLK_PER_TILE, body, 0)

    chain = [d for d in dils if d != BLOCKWISE]
    assert all(d % p == 0 for p, d in zip(chain, chain[1:])) and sum(d > 1 for d in chain[:-1]) <= 1
    for kk in range(len(dils)):
        pl.when(k == kk)(functools.partial(emit, kk))


def _prenorm(x2d, gain, dils):
    m, d_model = x2d.shape
    slabs = pltpu.VMEM((N_CHUNK, ROW_TILE, LANES), f32)
    per_row = pltpu.VMEM((ROW_TILE, LANES), f32)
    return pl.pallas_call(
        functools.partial(_prenorm_kernel, dils=dils),
        grid=(m // ROW_TILE, len(dils)),
        in_specs=[pl.BlockSpec((ROW_TILE, d_model), lambda t, k: (t, 0)),
                  pl.BlockSpec((1, d_model), lambda t, k: (0, 0))],
        out_specs=pl.BlockSpec((None, ROW_TILE, d_model), lambda t, k: (k, t, 0)),
        out_shape=jax.ShapeDtypeStruct((len(dils), m, d_model), bf16),
        scratch_shapes=[slabs, per_row, slabs, per_row],
        compiler_params=_cparams(2),
        name="prenorm",
    )(x2d, gain)


def _proj_kernel(u_ref, w_ref, cos_ref, sin_ref, o_ref, *, rope_secs):
    j = pl.program_id(1)
    n_c = o_ref.shape[0]
    lane = lax.broadcasted_iota(jnp.int32, (1, LANES), 1)
    first_half = (lane % HEAD_DIM) < (HEAD_DIM // 2)
    is_rope = functools.reduce(jnp.logical_or, [j == s for s in rope_secs])

    @pl.when(is_rope)
    def _():
        cos, sin = cos_ref[...], sin_ref[...]
        half = n_c // 2
        for part in range(2):
            cols = slice(part * half * LANES, (part + 1) * half * LANES)
            acc = jnp.dot(u_ref[...], w_ref[:, cols], preferred_element_type=f32)
            for c in range(half):
                x = acc[:, c * LANES:(c + 1) * LANES].astype(bf16)
                partner = jnp.where(first_half,
                                    pltpu.roll(x, LANES - HEAD_DIM // 2, axis=1),
                                    pltpu.roll(x, HEAD_DIM // 2, axis=1))
                o_ref[part * half + c] = x * cos + partner * sin

    @pl.when(jnp.logical_not(is_rope))
    def _():
        acc = jnp.dot(u_ref[...], w_ref[...], preferred_element_type=f32)
        for c in range(n_c):
            o_ref[c] = acc[:, c * LANES:(c + 1) * LANES].astype(bf16)


def _lookup(j, table):
    return sum([jnp.where(j == pos, val, 0) for pos, val in enumerate(table)], jnp.int32(0))


def _in_proj(u, w_bf16, cos_t, sin_t, batch, seq, visit, sec_dsel, q_steps, k_steps, bm=ROW_TILE):
    _, m, d_in = u.shape
    n_out = w_bf16.shape[1]
    bn = D_MODEL
    tiles_per_seq = seq // bm
    sec_tab = [N_ROPE_KINDS * d + (ROPE_Q if j in q_steps else ROPE_K) for j, d in enumerate(sec_dsel)]

    def sec(j):
        return _lookup(j, visit)

    def dsel(j):
        return _lookup(j, sec_dsel)

    def tsel(j):
        return _lookup(j, sec_tab)

    kern = functools.partial(_proj_kernel, rope_secs=sorted(q_steps + k_steps))
    return pl.pallas_call(
        kern,
        grid=(m // bm, n_out // bn),
        in_specs=[
            pl.BlockSpec((None, bm, d_in), lambda i, j: (dsel(j), i, 0)),
            pl.BlockSpec((d_in, bn), lambda i, j: (0, sec(j))),
            pl.BlockSpec((None, bm, LANES), lambda i, j: (tsel(j), i % tiles_per_seq, 0)),
            pl.BlockSpec((None, bm, LANES), lambda i, j: (tsel(j), i % tiles_per_seq, 0)),
        ],
        out_specs=pl.BlockSpec((None, bn // LANES, bm, LANES),
                               lambda i, j: (i // tiles_per_seq, sec(j), i % tiles_per_seq, 0)),
        out_shape=jax.ShapeDtypeStruct((batch, n_out // LANES, seq, LANES), bf16),
        compiler_params=_cparams(2),
        name="in_proj",
    )(u, w_bf16, cos_t, sin_t)


def _dil_attn_kernel(bias_ref, q0, k0, v0, q1, k1, v1, q2, k2, v2, gate_ref, y_ref,
                     num_sc, den_sc, max_sc):
    seq = y_ref.shape[0]
    n_blk = seq // WIN
    lane = lax.broadcasted_iota(jnp.int32, (1, LANES), 1)
    low = lane < HEAD_DIM
    quarter = ROW_TILE // MERGE_ORDER
    outs = (num_sc, den_sc, max_sc)
    sel0 = jnp.where(low, 1.0, 0.0).astype(bf16)
    sel1 = jnp.where(low, 0.0, 1.0).astype(bf16)
    ones = jnp.ones((2 * WIN, LANES), bf16)
    groups = ((q0, k0, v0), (q1, k1, v1), (q2, k2, v2))

    def block(g, blk):
        t, u = divmod(blk, BLK_PER_TILE)
        row0 = blk * WIN
        d, order = DILATIONS[g], ORDERS[g]
        assert order == (BLOCKWISE if d == 1 else d)
        q_ref, k_ref, v_ref = groups[g]
        per_res = BLK_PER_TILE // d
        r, n = divmod(u, per_res)
        prev = blk - 1 if n > 0 else blk - BLK_PER_TILE + per_res - 1
        has_prev = prev >= 0
        prow0 = max(prev, 0) * WIN

        q = q_ref[pl.ds(row0, WIN), :]
        qs = jnp.concatenate([q * sel0, q * sel1], axis=0)
        kcat = jnp.concatenate([k_ref[pl.ds(prow0, WIN), :], k_ref[pl.ds(row0, WIN), :]], axis=0)
        vcat = jnp.concatenate([v_ref[pl.ds(prow0, WIN), :], v_ref[pl.ds(row0, WIN), :]], axis=0)
        s = lax.dot_general(qs, kcat, (((1,), (1,)), ((), ())), preferred_element_type=f32)
        s = s + bias_ref[2 * int(order == BLOCKWISE) + int(has_prev)]
        m = jnp.max(s, axis=-1, keepdims=True)
        p = jnp.exp2(s - m).astype(bf16)
        vext = jnp.concatenate([vcat, ones], axis=1)
        pv = jnp.dot(p, vext, preferred_element_type=f32)
        vals = (jnp.where(low, pv[:WIN, :LANES], pv[WIN:, :LANES]),
                jnp.where(low, pv[:WIN, LANES:], pv[WIN:, LANES:]),
                jnp.where(low, m[:WIN], m[WIN:]))
        piece = WIN // MERGE_ORDER
        for out, val in zip(outs, vals):
            if order == BLOCKWISE:
                for b in range(MERGE_ORDER):
                    out[g, pl.ds(t * ROW_TILE + b * quarter + piece * u, piece), :] = (
                        val[b * piece:(b + 1) * piece])
            elif order == MERGE_ORDER:
                out[g, pl.ds(row0, WIN), :] = val
            else:
                a, b = divmod(r, MERGE_ORDER)
                step = d // MERGE_ORDER
                out[g, pl.ds(t * ROW_TILE + b * quarter + step * n * WIN + a, WIN, stride=step), :] = val

    def merge(i):
        rows = pl.ds(i * WIN, WIN)
        maxs = [max_sc[g, rows, :] for g in range(len(DILATIONS))]
        mx = functools.reduce(jnp.maximum, maxs)
        es = [jnp.exp2(m - mx) for m in maxs]
        num = sum(e * num_sc[g, rows, :] for g, e in enumerate(es))
        den = sum(e * den_sc[g, rows, :] for g, e in enumerate(es))
        gate = gate_ref[rows, :].astype(f32)
        y_ref[rows, :] = ((num * gate) / (den * (1.0 + jnp.exp(-gate)))).astype(bf16)

    def merge_blocks_of(g, u):
        if ORDERS[g] == BLOCKWISE:
            return {MERGE_ORDER * b + u // MERGE_ORDER for b in range(MERGE_ORDER)}
        if ORDERS[g] == MERGE_ORDER:
            return {u}
        b = (u // (BLK_PER_TILE // DILATIONS[g])) % MERGE_ORDER
        return {MERGE_ORDER * b + c for c in range(quarter // WIN)}

    def first_use(g, u):
        return min(max(divmod(i, MERGE_ORDER)) for i in merge_blocks_of(g, u))

    visit = [sorted(range(BLK_PER_TILE), key=lambda u, g=g: (first_use(g, u), u)) for g in range(len(DILATIONS))]
    done, pending = set(), list(range(n_blk))
    for t in range(n_blk // BLK_PER_TILE):
        for j in range(BLK_PER_TILE):
            for g in range(len(DILATIONS)):
                block(g, t * BLK_PER_TILE + visit[g][j])
                done.add((g, t, visit[g][j]))
            ready = [i for i in pending
                     if all((g, i // BLK_PER_TILE, u) in done for g in range(len(DILATIONS))
                            for u in range(BLK_PER_TILE) if i % BLK_PER_TILE in merge_blocks_of(g, u))]
            for i in ready[:2]:
                merge(i)
                pending.remove(i)
    for i in pending:
        merge(i)


def _dil_attention(proj, batch, seq, sec_of):
    def slab(sec):
        return pl.BlockSpec((None, None, seq, LANES), lambda b, hp: (b, sec * N_CHUNK + hp, 0, 0))

    bias = _window_bias()
    specs = [pl.BlockSpec(bias.shape, lambda b, hp: (0, 0, 0))]
    args = [bias]
    for g in range(len(DILATIONS)):
        for kind in "qkv":
            specs.append(slab(sec_of[kind + str(g)]))
            args.append(proj)
    specs.append(slab(sec_of["gate"]))
    args.append(proj)
    return pl.pallas_call(
        _dil_attn_kernel,
        grid=(batch, N_CHUNK),
        in_specs=specs,
        out_specs=pl.BlockSpec((None, seq, LANES), lambda b, hp: (b, 0, hp)),
        out_shape=jax.ShapeDtypeStruct((batch, seq, D_MODEL), bf16),
        scratch_shapes=[pltpu.VMEM((len(DILATIONS), seq, LANES), f32)] * 3,
        compiler_params=_cparams(2),
        name="dilated_attention",
    )(*args)


TAB_QI, TAB_KJ = range(2)


def _full_tiles(seq, tq, tk):
    pairs = [(qi, kj) for qi in range(seq // tq) for kj in range((qi * tq) // tk)]
    return list(zip(*pairs))


def _diff_attn_kernel(tab_ref, lq1_ref, lk1_ref, lq2_ref, lk2_ref, subln_ref, bias_ref,
                      q_ref, k_ref, v_ref, gate_ref, y_ref, acc_sc, m_sc, *,
                      tq, tk, n_full, unroll, lambda_init):
    lane = lax.broadcasted_iota(jnp.int32, (1, LANES), 1)
    low = lane < HEAD_DIM
    sel0 = jnp.where(low, 1.0, 0.0).astype(bf16)
    sel1 = jnp.where(low, 0.0, 1.0).astype(bf16)
    ones = jnp.ones((tk, LANES), bf16)
    lam = (jnp.exp(jnp.sum(lq1_ref[...] * lk1_ref[...], axis=-1, keepdims=True))
           - jnp.exp(jnp.sum(lq2_ref[...] * lk2_ref[...], axis=-1, keepdims=True)) + lambda_init)

    def scores(q0, k0, width):
        q = q_ref[pl.ds(q0, tq), :]
        qs = jnp.concatenate([q * sel0, q * sel1], axis=0)
        return lax.dot_general(qs, k_ref[pl.ds(k0, width), :], (((1,), (1,)), ((), ())),
                               preferred_element_type=f32)

    def weighted_values(p, k0, width):
        vext = jnp.concatenate([v_ref[pl.ds(k0, width), :], ones[:width]], axis=1)
        return jnp.dot(p, vext, preferred_element_type=f32)

    for qi in range(y_ref.shape[0] // tq):
        k0 = (qi * tq) // tk * tk
        width = (qi + 1) * tq - k0
        bias = bias_ref[(qi * tq - k0) // tq, :, :width]
        s = scores(qi * tq, k0, width) + jnp.tile(bias, (2, 1))
        m = jnp.max(s, axis=-1, keepdims=True)
        m_sc[qi] = jnp.broadcast_to(m, m_sc.shape[1:])
        acc_sc[qi] = weighted_values(jnp.exp2(s - m).astype(bf16), k0, width)

    def step(t):
        qi = tab_ref[TAB_QI, t]
        q0 = pl.multiple_of(qi * tq, tq)
        k0 = pl.multiple_of(tab_ref[TAB_KJ, t] * tk, tk)
        s = scores(q0, k0, tk)
        m_old = m_sc[qi]
        m_new = jnp.maximum(m_old, jnp.max(s, axis=-1, keepdims=True))
        alpha = jnp.exp2(m_old - m_new)
        m_sc[qi] = m_new
        p = jnp.exp2(s - jnp.tile(m_new, (1, tk // LANES))).astype(bf16)
        acc_sc[qi] = jnp.tile(alpha, (1, 2)) * acc_sc[qi] + weighted_values(p, k0, tk)

    def steps(i, carry):
        for k in range(unroll):
            step(unroll * i + k)
        return carry

    assert n_full % unroll == 0
    lax.fori_loop(0, n_full // unroll, steps, 0)

    def finalize(qi, carry):
        q0 = pl.multiple_of(qi * tq, tq)
        acc = acc_sc[qi]
        o = acc[:, :LANES] / acc[:, LANES:]
        o = o[:tq] - lam * o[tq:]
        o = o * lax.rsqrt(jnp.mean(o * o, axis=-1, keepdims=True) + EPS)
        o = o * subln_ref[...] * (1.0 - lambda_init)
        gate = gate_ref[pl.ds(q0, tq), :].astype(f32)
        y_ref[pl.ds(q0, tq), :] = (o * (gate * jax.nn.sigmoid(gate))).astype(bf16)
        return carry

    lax.fori_loop(0, y_ref.shape[0] // tq, finalize, 0, unroll=2)


def _diff_attention(proj, lq1, lk1, lq2, lk2, subln, batch, seq, lambda_init, tq, tk):
    def slab(c0):
        return pl.BlockSpec((None, None, seq, LANES), lambda b, h: (b, c0 + h, 0, 0))

    def small(a):
        return pl.BlockSpec(a.shape, lambda b, h: (0,) * a.ndim)

    table = jnp.array(_full_tiles(seq, tq, tk), jnp.int32)
    n_full = table.shape[1]
    qi = jnp.arange(tq)[:, None]
    kj = jnp.arange(tk)[None, :]
    bias = jnp.stack([jnp.where(kj <= qi + o * tq, 0.0, NEG).astype(f32) for o in range(tk // tq)])
    smalls = [lq1, lk1, lq2, lk2, subln, bias]
    kern = functools.partial(_diff_attn_kernel, tq=tq, tk=tk, n_full=n_full, unroll=28,
                             lambda_init=lambda_init)
    return pl.pallas_call(
        kern,
        grid=(batch, N_CHUNK),
        in_specs=([pl.BlockSpec(memory_space=pltpu.SMEM)] + [small(a) for a in smalls]
                  + [slab(0), slab(N_CHUNK), slab(2 * N_CHUNK), slab(3 * N_CHUNK)]),
        out_specs=pl.BlockSpec((None, seq, LANES), lambda b, h: (b, 0, h)),
        out_shape=jax.ShapeDtypeStruct((batch, seq, D_MODEL), bf16),
        scratch_shapes=[pltpu.VMEM((seq // tq, 2 * tq, 2 * LANES), f32),
                        pltpu.VMEM((seq // tq, 2 * tq, LANES), f32)],
        compiler_params=_cparams(2),
        name="diff_attention",
    )(table, *smalls, proj, proj, proj, proj)


def _out_proj_kernel(y_ref, w_ref, g_ref, h_ref, o_ref):
    z = jnp.dot(y_ref[...], w_ref[...], preferred_element_type=f32)
    z = z * lax.rsqrt(jnp.mean(z * z, axis=-1, keepdims=True) + EPS)
    o_ref[...] = h_ref[...] + z * g_ref[...]


def _out_proj_reorder_kernel(y_ref, w_ref, g_ref, h_ref, g_next_ref, o_ref, u_ref, z_sc):
    q, part, d = y_ref.shape
    bm = q * part
    z = jnp.dot(y_ref[...].reshape(bm, d), w_ref[...], preferred_element_type=f32)
    z = (z * lax.rsqrt(jnp.mean(z * z, axis=-1, keepdims=True) + EPS)) * g_ref[...]
    for c in range(d // LANES):
        for b in range(q):
            for j0 in range(0, part, WIN):
                z_sc[c, pl.ds(q * j0 + b, WIN, stride=q), :] = (
                    z[b * part + j0:b * part + j0 + WIN, c * LANES:(c + 1) * LANES])
    sq = jnp.zeros((bm, LANES), f32)
    for c in range(d // LANES):
        cols = slice(c * LANES, (c + 1) * LANES)
        h = h_ref[:, cols] + z_sc[c]
        o_ref[:, cols] = h
        sq = sq + h * h
    rs = lax.rsqrt(jnp.sum(sq, axis=-1, keepdims=True) * (1.0 / d) + EPS)
    for c in range(d // LANES):
        cols = slice(c * LANES, (c + 1) * LANES)
        u_ref[:, cols] = ((o_ref[:, cols] * rs) * g_next_ref[:, cols]).astype(bf16)


def _out_proj(y2d, w_bf16, gain, h2d, next_gain=None, y_order=1, bm=1024):
    m, d = h2d.shape
    row = pl.BlockSpec((bm, d), lambda i: (i, 0))
    vec = pl.BlockSpec((1, d), lambda i: (0, 0))
    in_specs = [row, pl.BlockSpec((d, d), lambda i: (0, 0)), vec, row]
    assert (y_order > 1) == (next_gain is not None)
    if y_order > 1:
        q, steps_per_tile = y_order, ROW_TILE // bm
        part = bm // q
        y5d = y2d.reshape(m // ROW_TILE, q, steps_per_tile, part, d)
        in_specs[0] = pl.BlockSpec((None, q, None, part, d),
                                   lambda i: (i // steps_per_tile, 0, i % steps_per_tile, 0, 0))
        return pl.pallas_call(
            _out_proj_reorder_kernel, grid=(m // bm,), in_specs=in_specs + [vec],
            out_specs=[row, pl.BlockSpec((None, bm, d), lambda i: (0, i, 0))],
            out_shape=[jax.ShapeDtypeStruct((m, d), f32), jax.ShapeDtypeStruct((1, m, d), bf16)],
            scratch_shapes=[pltpu.VMEM((d // LANES, bm, LANES), f32)],
            compiler_params=_cparams(1), name="out_proj_reorder",
        )(y5d, w_bf16, gain, h2d, next_gain)
    return pl.pallas_call(
        _out_proj_kernel, grid=(m // bm,), in_specs=in_specs, out_specs=row,
        out_shape=jax.ShapeDtypeStruct((m, d), f32),
        compiler_params=_cparams(1), name="out_proj",
    )(y2d, w_bf16, gain, h2d)


def _rope_tables(seq, dils):
    half = HEAD_DIM // 2
    freqs = ROPE_THETA ** (-jnp.arange(0, HEAD_DIM, 2, dtype=f32) / HEAD_DIM)
    lane = jnp.arange(LANES)
    freq_l = freqs[lane % half][None, None, :]
    sign_l = jnp.where(lane % HEAD_DIM < half, -1.0, 1.0).astype(f32)[None, None, :]
    row = jnp.arange(seq)

    def position(order):
        tile, d = (WIN, 4) if order == BLOCKWISE else (ROW_TILE, order)
        w = row % tile
        return row - w + (w % (tile // d)) * d + w // (tile // d)

    pos = jnp.stack([position(order) for order in dils])
    ang = pos.astype(f32)[:, None, :, None] * freq_l
    kind_scale = jnp.array([1.0, Q_SCALE], f32)[None, :, None, None]
    cos = (jnp.cos(ang) * kind_scale).astype(bf16)
    sin = (sign_l * jnp.sin(ang) * kind_scale).astype(bf16)
    return cos.reshape(-1, seq, LANES), sin.reshape(-1, seq, LANES)


def _window_bias():
    def masks(step_of_row):
        qi = step_of_row[jnp.arange(2 * WIN) % WIN][:, None]
        kj = jnp.arange(2 * WIN)[None, :]
        kstep = step_of_row[kj % WIN]
        cur = (kj >= WIN) & (kstep <= qi)
        prev = (kj < WIN) & (kstep >= qi)
        return [jnp.where(cur, 0.0, NEG), jnp.where(cur | prev, 0.0, NEG)]

    rows = jnp.arange(WIN)
    piece = WIN // 4
    return jnp.stack(masks(rows) + masks((rows % piece) * 4 + rows // piece)).astype(f32)


def kernel(x, norm_pre, norm_post, dil_w_in, dil_w_out, diff_w_in, diff_w_out, diff_lambda_q1, diff_lambda_k1, diff_lambda_q2, diff_lambda_k2, diff_subln):
    batch, seq, d_model = x.shape
    assert d_model == D_MODEL and seq % ROW_TILE == 0
    n_g = len(DILATIONS)
    h = x.reshape(batch * seq, d_model)

    names = [kind + str(g) for kind in "qkv" for g in range(n_g)] + ["gate"]
    order_idx = [ORDERS.index(MERGE_ORDER) if name == "gate" else int(name[1]) for name in names]
    visit = sorted(range(len(names)), key=lambda s: (order_idx[s], s))
    sec_of = {name: s for s, name in enumerate(names)}
    sec_dsel = [order_idx[s] for s in visit]
    q_steps = [j for j, s in enumerate(visit) if names[s][0] == "q"]
    k_steps = [j for j, s in enumerate(visit) if names[s][0] == "k"]
    cos_t, sin_t = _rope_tables(seq, ORDERS)
    u = _prenorm(h, norm_pre[0][None, :], ORDERS)
    proj = _in_proj(u, dil_w_in[0].astype(bf16), cos_t, sin_t, batch, seq, visit, sec_dsel, q_steps, k_steps)
    y = _dil_attention(proj, batch, seq, sec_of)
    h, u = _out_proj(y.reshape(batch * seq, d_model), dil_w_out[0].astype(bf16), norm_post[0][None, :], h,
                     next_gain=norm_pre[1][None, :], y_order=MERGE_ORDER)

    lambda_init = 0.8 - 0.6 * math.exp(-0.3 * 1)
    tq, tk = 256, 512
    cos_t, sin_t = _rope_tables(seq, (1,))
    proj = _in_proj(u, diff_w_in[0].astype(bf16), cos_t, sin_t, batch, seq, [0, 1, 2, 3], [0, 0, 0, 0], [0], [1])
    y = _diff_attention(proj, diff_lambda_q1[0][None, :], diff_lambda_k1[0][None, :],
                        diff_lambda_q2[0][None, :], diff_lambda_k2[0][None, :],
                        diff_subln[0][None, :], batch, seq, lambda_init, tq, tk)
    h = _out_proj(y.reshape(batch * seq, d_model), diff_w_out[0].astype(bf16), norm_post[1][None, :], h)
    return h.reshape(batch, seq, d_model)
```

```python
import functools
import math

import jax
import jax.numpy as jnp
from jax import lax
from jax.experimental import pallas as pl
from jax.experimental.pallas import tpu as pltpu

D_MODEL = 1024
HEAD_DIM = 64
ROPE_THETA = 10000.0
EPS = 1e-6
DILATIONS = (1, 4, 16)
WIN = 128
LANES = 128
N_CHUNK = D_MODEL // LANES
ROW_TILE = 2048
BLK_PER_TILE = ROW_TILE // WIN
NEG = -1e30
Q_SCALE = HEAD_DIM ** -0.5 * math.log2(math.e)
BLOCKWISE = 0
ORDERS = (BLOCKWISE, 4, 16)
MERGE_ORDER = 4
ROPE_K, ROPE_Q, N_ROPE_KINDS = 0, 1, 2
VMEM_LIMIT = 52 * 1024 * 1024

f32 = jnp.float32
bf16 = jnp.bfloat16


def _cparams(n_axes):
    return pltpu.CompilerParams(
        dimension_semantics=("arbitrary",) * n_axes, vmem_limit_bytes=VMEM_LIMIT)


def _prenorm_kernel(x_ref, g_ref, o_ref, xn_sc, rs_sc, xp_sc, rsp_sc, *, dils):
    k = pl.program_id(1)
    d_model = N_CHUNK * LANES
    sub = 2 * WIN

    @pl.when(k == 0)
    def _():
        def scale(i, carry):
            rows = pl.ds(pl.multiple_of(i * sub, sub), sub)
            sq = jnp.zeros((sub, LANES), f32)
            for c in range(N_CHUNK):
                x = x_ref[rows, c * LANES:(c + 1) * LANES]
                xn_sc[c, rows, :] = x
                sq = sq + x * x
            ss = jnp.sum(sq, axis=-1, keepdims=True)
            rs_sc[rows, :] = jnp.broadcast_to(lax.rsqrt(ss * (1.0 / d_model) + EPS), (sub, LANES))
            return carry

        lax.fori_loop(0, ROW_TILE // sub, scale, 0)

    def emit_blockwise():
        def body(s, carry):
            for r in range(4):
                rows = pl.ds(s * WIN + r, WIN // 4, stride=4)
                dst = pl.ds(pl.multiple_of(s * WIN + r * (WIN // 4), WIN // 4), WIN // 4)
                rs = rs_sc[rows, :]
                for c in range(N_CHUNK):
                    cols = slice(c * LANES, (c + 1) * LANES)
                    o_ref[dst, cols] = ((xn_sc[c, rows, :] * rs) * g_ref[:, cols]).astype(bf16)
            return carry

        lax.fori_loop(0, BLK_PER_TILE, body, 0)

    def emit(kk):
        d = dils[kk]
        if d == BLOCKWISE:
            return emit_blockwise()
        d_prev = dils[kk - 1] if kk and dils[kk - 1] != BLOCKWISE else 1
        q = d // d_prev
        from_copy = d_prev > 1
        keep_copy = kk + 1 < len(dils) and d > 1
        per_res = BLK_PER_TILE // d

        def body(s, carry):
            r, n = s // per_res, s % per_res
            src = (r % d_prev) * (ROW_TILE // d_prev) + r // d_prev + q * n * WIN
            rows = pl.ds(src, WIN, stride=q) if q > 1 else pl.ds(pl.multiple_of(src, WIN), WIN)
            dst = pl.ds(pl.multiple_of(s * WIN, WIN), WIN)
            rs = (rsp_sc if from_copy else rs_sc)[rows, :]
            if keep_copy:
                rsp_sc[dst, :] = rs
            for c in range(N_CHUNK):
                x = (xp_sc if from_copy else xn_sc)[c, rows, :]
                if keep_copy:
                    xp_sc[c, dst, :] = x
                cols = slice(c * LANES, (c + 1) * LANES)
                o_ref[dst, cols] = ((x * rs) * g_ref[:, cols]).astype(bf16)
            return carry

        lax.fori_loop(0, BLK_PER_TILE, body, 0)

    chain = [d for d in dils if d != BLOCKWISE]
    assert all(d % p == 0 for p, d in zip(chain, chain[1:])) and sum(d > 1 for d in chain[:-1]) <= 1
    for kk in range(len(dils)):
        pl.when(k == kk)(functools.partial(emit, kk))


def _prenorm(x2d, gain, dils):
    m, d_model = x2d.shape
    slabs = pltpu.VMEM((N_CHUNK, ROW_TILE, LANES), f32)
    per_row = pltpu.VMEM((ROW_TILE, LANES), f32)
    return pl.pallas_call(
        functools.partial(_prenorm_kernel, dils=dils),
        grid=(m // ROW_TILE, len(dils)),
        in_specs=[pl.BlockSpec((ROW_TILE, d_model), lambda t, k: (t, 0)),
                  pl.BlockSpec((1, d_model), lambda t, k: (0, 0))],
        out_specs=pl.BlockSpec((None, ROW_TILE, d_model), lambda t, k: (k, t, 0)),
        out_shape=jax.ShapeDtypeStruct((len(dils), m, d_model), bf16),
        scratch_shapes=[slabs, per_row, slabs, per_row],
        compiler_params=_cparams(2),
        name="prenorm",
    )(x2d, gain)


def _proj_kernel(u_ref, w_ref, cos_ref, sin_ref, o_ref, *, rope_secs):
    j = pl.program_id(1)
    n_c = o_ref.shape[0]
    lane = lax.broadcasted_iota(jnp.int32, (1, LANES), 1)
    first_half = (lane % HEAD_DIM) < (HEAD_DIM // 2)
    is_rope = functools.reduce(jnp.logical_or, [j == s for s in rope_secs])

    @pl.when(is_rope)
    def _():
        half = n_c // 2
        quarter_rows = u_ref.shape[0] // 4
        for rq in range(4):
            rows = slice(rq * quarter_rows, (rq + 1) * quarter_rows)
            cos, sin = cos_ref[rows, :], sin_ref[rows, :]
            for part in range(2):
                cols = slice(part * half * LANES, (part + 1) * half * LANES)
                acc = jnp.dot(u_ref[rows, :], w_ref[:, cols], preferred_element_type=f32)
                for c in range(half):
                    x = acc[:, c * LANES:(c + 1) * LANES].astype(bf16)
                    partner = jnp.where(first_half,
                                        pltpu.roll(x, LANES - HEAD_DIM // 2, axis=1),
                                        pltpu.roll(x, HEAD_DIM // 2, axis=1))
                    o_ref[part * half + c, rows, :] = x * cos + partner * sin

    @pl.when(jnp.logical_not(is_rope))
    def _():
        acc = jnp.dot(u_ref[...], w_ref[...], preferred_element_type=f32)
        for c in range(n_c):
            o_ref[c] = acc[:, c * LANES:(c + 1) * LANES].astype(bf16)


def _lookup(j, table):
    return sum([jnp.where(j == pos, val, 0) for pos, val in enumerate(table)], jnp.int32(0))


def _in_proj(u, w_bf16, cos_t, sin_t, batch, seq, visit, sec_dsel, q_steps, k_steps, bm=ROW_TILE):
    _, m, d_in = u.shape
    n_out = w_bf16.shape[1]
    bn = D_MODEL
    tiles_per_seq = seq // bm
    sec_tab = [N_ROPE_KINDS * d + (ROPE_Q if j in q_steps else ROPE_K) for j, d in enumerate(sec_dsel)]

    def sec(j):
        return _lookup(j, visit)

    def dsel(j):
        return _lookup(j, sec_dsel)

    def tsel(j):
        return _lookup(j, sec_tab)

    kern = functools.partial(_proj_kernel, rope_secs=sorted(q_steps + k_steps))
    return pl.pallas_call(
        kern,
        grid=(m // bm, n_out // bn),
        in_specs=[
            pl.BlockSpec((None, bm, d_in), lambda i, j: (dsel(j), i, 0)),
            pl.BlockSpec((d_in, bn), lambda i, j: (0, sec(j))),
            pl.BlockSpec((None, bm, LANES), lambda i, j: (tsel(j), i % tiles_per_seq, 0)),
            pl.BlockSpec((None, bm, LANES), lambda i, j: (tsel(j), i % tiles_per_seq, 0)),
        ],
        out_specs=pl.BlockSpec((None, bn // LANES, bm, LANES),
                               lambda i, j: (i // tiles_per_seq, sec(j), i % tiles_per_seq, 0)),
        out_shape=jax.ShapeDtypeStruct((batch, n_out // LANES, seq, LANES), bf16),
        compiler_params=_cparams(2),
        name="in_proj",
    )(u, w_bf16, cos_t, sin_t)


def _dil_attn_kernel(bias_ref, q0, k0, v0, q1, k1, v1, q2, k2, v2, gate_ref, y_ref,
                     num_sc, den_sc, max_sc):
    seq = y_ref.shape[0]
    n_blk = seq // WIN
    lane = lax.broadcasted_iota(jnp.int32, (1, LANES), 1)
    low = lane < HEAD_DIM
    quarter = ROW_TILE // MERGE_ORDER
    outs = (num_sc, den_sc, max_sc)
    sel0 = jnp.where(low, 1.0, 0.0).astype(bf16)
    sel1 = jnp.where(low, 0.0, 1.0).astype(bf16)
    ones = jnp.ones((2 * WIN, LANES), bf16)
    groups = ((q0, k0, v0), (q1, k1, v1), (q2, k2, v2))

    def block(g, blk):
        t, u = divmod(blk, BLK_PER_TILE)
        row0 = blk * WIN
        d, order = DILATIONS[g], ORDERS[g]
        assert order == (BLOCKWISE if d == 1 else d)
        q_ref, k_ref, v_ref = groups[g]
        per_res = BLK_PER_TILE // d
        r, n = divmod(u, per_res)
        prev = blk - 1 if n > 0 else blk - BLK_PER_TILE + per_res - 1
        has_prev = prev >= 0
        prow0 = max(prev, 0) * WIN

        q = q_ref[pl.ds(row0, WIN), :]
        qs = jnp.concatenate([q * sel0, q * sel1], axis=0)
        kcat = jnp.concatenate([k_ref[pl.ds(prow0, WIN), :], k_ref[pl.ds(row0, WIN), :]], axis=0)
        vcat = jnp.concatenate([v_ref[pl.ds(prow0, WIN), :], v_ref[pl.ds(row0, WIN), :]], axis=0)
        s = lax.dot_general(qs, kcat, (((1,), (1,)), ((), ())), preferred_element_type=f32)
        s = s + bias_ref[2 * int(order == BLOCKWISE) + int(has_prev)]
        m = jnp.max(s, axis=-1, keepdims=True)
        p = jnp.exp2(s - m).astype(bf16)
        vext = jnp.concatenate([vcat, ones], axis=1)
        pv = jnp.dot(p, vext, preferred_element_type=f32)
        vals = (jnp.where(low, pv[:WIN, :LANES], pv[WIN:, :LANES]),
                jnp.where(low, pv[:WIN, LANES:], pv[WIN:, LANES:]),
                jnp.where(low, m[:WIN], m[WIN:]))
        piece = WIN // MERGE_ORDER
        for out, val in zip(outs, vals):
            if order == BLOCKWISE:
                for b in range(MERGE_ORDER):
                    out[g, pl.ds(t * ROW_TILE + b * quarter + piece * u, piece), :] = (
                        val[b * piece:(b + 1) * piece])
            elif order == MERGE_ORDER:
                out[g, pl.ds(row0, WIN), :] = val
            else:
                a, b = divmod(r, MERGE_ORDER)
                step = d // MERGE_ORDER
                out[g, pl.ds(t * ROW_TILE + b * quarter + step * n * WIN + a, WIN, stride=step), :] = val

    def merge(i):
        rows = pl.ds(i * WIN, WIN)
        maxs = [max_sc[g, rows, :] for g in range(len(DILATIONS))]
        mx = functools.reduce(jnp.maximum, maxs)
        es = [jnp.exp2(m - mx) for m in maxs]
        num = sum(e * num_sc[g, rows, :] for g, e in enumerate(es))
        den = sum(e * den_sc[g, rows, :] for g, e in enumerate(es))
        gate = gate_ref[rows, :].astype(f32)
        y_ref[rows, :] = ((num * gate) / (den * (1.0 + jnp.exp(-gate)))).astype(bf16)

    def merge_blocks_of(g, u):
        if ORDERS[g] == BLOCKWISE:
            return {MERGE_ORDER * b + u // MERGE_ORDER for b in range(MERGE_ORDER)}
        if ORDERS[g] == MERGE_ORDER:
            return {u}
        b = (u // (BLK_PER_TILE // DILATIONS[g])) % MERGE_ORDER
        return {MERGE_ORDER * b + c for c in range(quarter // WIN)}

    def first_use(g, u):
        return min(max(divmod(i, MERGE_ORDER)) for i in merge_blocks_of(g, u))

    visit = [sorted(range(BLK_PER_TILE), key=lambda u, g=g: (first_use(g, u), u)) for g in range(len(DILATIONS))]
    done, pending = set(), list(range(n_blk))
    for t in range(n_blk // BLK_PER_TILE):
        for j in range(BLK_PER_TILE):
            for g in range(len(DILATIONS)):
                block(g, t * BLK_PER_TILE + visit[g][j])
                done.add((g, t, visit[g][j]))
            ready = [i for i in pending
                     if all((g, i // BLK_PER_TILE, u) in done for g in range(len(DILATIONS))
                            for u in range(BLK_PER_TILE) if i % BLK_PER_TILE in merge_blocks_of(g, u))]
            for i in ready[:2]:
                merge(i)
                pending.remove(i)
    for i in pending:
        merge(i)


def _dil_attention(proj, batch, seq, sec_of):
    def slab(sec):
        return pl.BlockSpec((None, None, seq, LANES), lambda b, hp: (b, sec * N_CHUNK + hp, 0, 0))

    bias = _window_bias()
    specs = [pl.BlockSpec(bias.shape, lambda b, hp: (0, 0, 0))]
    args = [bias]
    for g in range(len(DILATIONS)):
        for kind in "qkv":
            specs.append(slab(sec_of[kind + str(g)]))
            args.append(proj)
    specs.append(slab(sec_of["gate"]))
    args.append(proj)
    return pl.pallas_call(
        _dil_attn_kernel,
        grid=(batch, N_CHUNK),
        in_specs=specs,
        out_specs=pl.BlockSpec((None, seq, LANES), lambda b, hp: (b, 0, hp)),
        out_shape=jax.ShapeDtypeStruct((batch, seq, D_MODEL), bf16),
        scratch_shapes=[pltpu.VMEM((len(DILATIONS), seq, LANES), f32)] * 3,
        compiler_params=_cparams(2),
        name="dilated_attention",
    )(*args)


TAB_QI, TAB_KJ = range(2)


def _full_tiles(seq, tq, tk):
    pairs = [(qi, kj) for qi in range(seq // tq) for kj in range((qi * tq) // tk)]
    return list(zip(*pairs))


def _diff_attn_kernel(tab_ref, lq1_ref, lk1_ref, lq2_ref, lk2_ref, subln_ref, bias_ref,
                      q_ref, k_ref, v_ref, gate_ref, y_ref, acc_sc, m_sc, *,
                      tq, tk, n_full, unroll, lambda_init):
    lane = lax.broadcasted_iota(jnp.int32, (1, LANES), 1)
    low = lane < HEAD_DIM
    sel0 = jnp.where(low, 1.0, 0.0).astype(bf16)
    sel1 = jnp.where(low, 0.0, 1.0).astype(bf16)
    ones = jnp.ones((tk, LANES), bf16)
    lam = (jnp.exp(jnp.sum(lq1_ref[...] * lk1_ref[...], axis=-1, keepdims=True))
           - jnp.exp(jnp.sum(lq2_ref[...] * lk2_ref[...], axis=-1, keepdims=True)) + lambda_init)

    def scores(q0, k0, width):
        q = q_ref[pl.ds(q0, tq), :]
        qs = jnp.concatenate([q * sel0, q * sel1], axis=0)
        return lax.dot_general(qs, k_ref[pl.ds(k0, width), :], (((1,), (1,)), ((), ())),
                               preferred_element_type=f32)

    def weighted_values(p, k0, width):
        vext = jnp.concatenate([v_ref[pl.ds(k0, width), :], ones[:width]], axis=1)
        return jnp.dot(p, vext, preferred_element_type=f32)

    for qi in range(y_ref.shape[0] // tq):
        k0 = (qi * tq) // tk * tk
        width = (qi + 1) * tq - k0
        bias = bias_ref[(qi * tq - k0) // tq, :, :width]
        s = scores(qi * tq, k0, width) + jnp.tile(bias, (2, 1))
        m = jnp.max(s, axis=-1, keepdims=True)
        m_sc[qi] = jnp.broadcast_to(m, m_sc.shape[1:])
        acc_sc[qi] = weighted_values(jnp.exp2(s - m).astype(bf16), k0, width)

    def step(t):
        qi = tab_ref[TAB_QI, t]
        q0 = pl.multiple_of(qi * tq, tq)
        k0 = pl.multiple_of(tab_ref[TAB_KJ, t] * tk, tk)
        s = scores(q0, k0, tk)
        m_old = m_sc[qi]
        m_new = jnp.maximum(m_old, jnp.max(s, axis=-1, keepdims=True))
        alpha = jnp.exp2(m_old - m_new)
        m_sc[qi] = m_new
        p = jnp.exp2(s - jnp.tile(m_new, (1, tk // LANES))).astype(bf16)
        acc_sc[qi] = jnp.tile(alpha, (1, 2)) * acc_sc[qi] + weighted_values(p, k0, tk)

    def steps(i, carry):
        for k in range(unroll):
            step(unroll * i + k)
        return carry

    assert n_full % unroll == 0
    lax.fori_loop(0, n_full // unroll, steps, 0)

    def finalize(qi, carry):
        q0 = pl.multiple_of(qi * tq, tq)
        acc = acc_sc[qi]
        o = acc[:, :LANES] / acc[:, LANES:]
        o = o[:tq] - lam * o[tq:]
        o = o * lax.rsqrt(jnp.mean(o * o, axis=-1, keepdims=True) + EPS)
        o = o * subln_ref[...] * (1.0 - lambda_init)
        gate = gate_ref[pl.ds(q0, tq), :].astype(f32)
        y_ref[pl.ds(q0, tq), :] = (o * (gate * jax.nn.sigmoid(gate))).astype(bf16)
        return carry

    lax.fori_loop(0, y_ref.shape[0] // tq, finalize, 0, unroll=2)


def _diff_attention(proj, lq1, lk1, lq2, lk2, subln, batch, seq, lambda_init, tq, tk):
    def slab(c0):
        return pl.BlockSpec((None, None, seq, LANES), lambda b, h: (b, c0 + h, 0, 0))

    def small(a):
        return pl.BlockSpec(a.shape, lambda b, h: (0,) * a.ndim)

    table = jnp.array(_full_tiles(seq, tq, tk), jnp.int32)
    n_full = table.shape[1]
    qi = jnp.arange(tq)[:, None]
    kj = jnp.arange(tk)[None, :]
    bias = jnp.stack([jnp.where(kj <= qi + o * tq, 0.0, NEG).astype(f32) for o in range(tk // tq)])
    smalls = [lq1, lk1, lq2, lk2, subln, bias]
    kern = functools.partial(_diff_attn_kernel, tq=tq, tk=tk, n_full=n_full, unroll=28,
                             lambda_init=lambda_init)
    return pl.pallas_call(
        kern,
        grid=(batch, N_CHUNK),
        in_specs=([pl.BlockSpec(memory_space=pltpu.SMEM)] + [small(a) for a in smalls]
                  + [slab(0), slab(N_CHUNK), slab(2 * N_CHUNK), slab(3 * N_CHUNK)]),
        out_specs=pl.BlockSpec((None, seq, LANES), lambda b, h: (b, 0, h)),
        out_shape=jax.ShapeDtypeStruct((batch, seq, D_MODEL), bf16),
        scratch_shapes=[pltpu.VMEM((seq // tq, 2 * tq, 2 * LANES), f32),
                        pltpu.VMEM((seq // tq, 2 * tq, LANES), f32)],
        compiler_params=_cparams(2),
        name="diff_attention",
    )(table, *smalls, proj, proj, proj, proj)


def _out_proj_kernel(y_ref, w_ref, g_ref, h_ref, o_ref):
    z = jnp.dot(y_ref[...], w_ref[...], preferred_element_type=f32)
    z = z * lax.rsqrt(jnp.mean(z * z, axis=-1, keepdims=True) + EPS)
    o_ref[...] = h_ref[...] + z * g_ref[...]


def _out_proj_reorder_kernel(y_ref, w_ref, g_ref, h_ref, g_next_ref, o_ref, u_ref, z_sc):
    q, part, d = y_ref.shape
    bm = q * part
    z = jnp.dot(y_ref[...].reshape(bm, d), w_ref[...], preferred_element_type=f32)
    z = (z * lax.rsqrt(jnp.mean(z * z, axis=-1, keepdims=True) + EPS)) * g_ref[...]
    for c in range(d // LANES):
        for b in range(q):
            for j0 in range(0, part, WIN):
                z_sc[c, pl.ds(q * j0 + b, WIN, stride=q), :] = (
                    z[b * part + j0:b * part + j0 + WIN, c * LANES:(c + 1) * LANES])
    sq = jnp.zeros((bm, LANES), f32)
    for c in range(d // LANES):
        cols = slice(c * LANES, (c + 1) * LANES)
        h = h_ref[:, cols] + z_sc[c]
        o_ref[:, cols] = h
        sq = sq + h * h
    rs = lax.rsqrt(jnp.sum(sq, axis=-1, keepdims=True) * (1.0 / d) + EPS)
    for c in range(d // LANES):
        cols = slice(c * LANES, (c + 1) * LANES)
        u_ref[:, cols] = ((o_ref[:, cols] * rs) * g_next_ref[:, cols]).astype(bf16)


def _out_proj(y2d, w_bf16, gain, h2d, next_gain=None, y_order=1, bm=1024):
    m, d = h2d.shape
    row = pl.BlockSpec((bm, d), lambda i: (i, 0))
    vec = pl.BlockSpec((1, d), lambda i: (0, 0))
    in_specs = [row, pl.BlockSpec((d, d), lambda i: (0, 0)), vec, row]
    assert (y_order > 1) == (next_gain is not None)
    if y_order > 1:
        q, steps_per_tile = y_order, ROW_TILE // bm
        part = bm // q
        y5d = y2d.reshape(m // ROW_TILE, q, steps_per_tile, part, d)
        in_specs[0] = pl.BlockSpec((None, q, None, part, d),
                                   lambda i: (i // steps_per_tile, 0, i % steps_per_tile, 0, 0))
        return pl.pallas_call(
            _out_proj_reorder_kernel, grid=(m // bm,), in_specs=in_specs + [vec],
            out_specs=[row, pl.BlockSpec((None, bm, d), lambda i: (0, i, 0))],
            out_shape=[jax.ShapeDtypeStruct((m, d), f32), jax.ShapeDtypeStruct((1, m, d), bf16)],
            scratch_shapes=[pltpu.VMEM((d // LANES, bm, LANES), f32)],
            compiler_params=_cparams(1), name="out_proj_reorder",
        )(y5d, w_bf16, gain, h2d, next_gain)
    return pl.pallas_call(
        _out_proj_kernel, grid=(m // bm,), in_specs=in_specs, out_specs=row,
        out_shape=jax.ShapeDtypeStruct((m, d), f32),
        compiler_params=_cparams(1), name="out_proj",
    )(y2d, w_bf16, gain, h2d)


def _rope_tables(seq, dils):
    half = HEAD_DIM // 2
    freqs = ROPE_THETA ** (-jnp.arange(0, HEAD_DIM, 2, dtype=f32) / HEAD_DIM)
    lane = jnp.arange(LANES)
    freq_l = freqs[lane % half][None, None, :]
    sign_l = jnp.where(lane % HEAD_DIM < half, -1.0, 1.0).astype(f32)[None, None, :]
    row = jnp.arange(seq)

    def position(order):
        tile, d = (WIN, 4) if order == BLOCKWISE else (ROW_TILE, order)
        w = row % tile
        return row - w + (w % (tile // d)) * d + w // (tile // d)

    pos = jnp.stack([position(order) for order in dils])
    ang = pos.astype(f32)[:, None, :, None] * freq_l
    kind_scale = jnp.array([1.0, Q_SCALE], f32)[None, :, None, None]
    cos = (jnp.cos(ang) * kind_scale).astype(bf16)
    sin = (sign_l * jnp.sin(ang) * kind_scale).astype(bf16)
    return cos.reshape(-1, seq, LANES), sin.reshape(-1, seq, LANES)


def _window_bias():
    def masks(step_of_row):
        qi = step_of_row[jnp.arange(2 * WIN) % WIN][:, None]
        kj = jnp.arange(2 * WIN)[None, :]
        kstep = step_of_row[kj % WIN]
        cur = (kj >= WIN) & (kstep <= qi)
        prev = (kj < WIN) & (kstep >= qi)
        return [jnp.where(cur, 0.0, NEG), jnp.where(cur | prev, 0.0, NEG)]

    rows = jnp.arange(WIN)
    piece = WIN // 4
    return jnp.stack(masks(rows) + masks((rows % piece) * 4 + rows // piece)).astype(f32)


def kernel(x, norm_pre, norm_post, dil_w_in, dil_w_out, diff_w_in, diff_w_out, diff_lambda_q1, diff_lambda_k1, diff_lambda_q2, diff_lambda_k2, diff_subln):
    batch, seq, d_model = x.shape
    assert d_model == D_MODEL and seq % ROW_TILE == 0
    n_g = len(DILATIONS)
    h = x.reshape(batch * seq, d_model)

    names = [kind + str(g) for kind in "qkv" for g in range(n_g)] + ["gate"]
    order_idx = [ORDERS.index(MERGE_ORDER) if name == "gate" else int(name[1]) for name in names]
    visit = sorted(range(len(names)), key=lambda s: (order_idx[s], s))
    sec_of = {name: s for s, name in enumerate(names)}
    sec_dsel = [order_idx[s] for s in visit]
    q_steps = [j for j, s in enumerate(visit) if names[s][0] == "q"]
    k_steps = [j for j, s in enumerate(visit) if names[s][0] == "k"]
    cos_t, sin_t = _rope_tables(seq, ORDERS)
    u = _prenorm(h, norm_pre[0][None, :], ORDERS)
    proj = _in_proj(u, dil_w_in[0].astype(bf16), cos_t, sin_t, batch, seq, visit, sec_dsel, q_steps, k_steps)
    y = _dil_attention(proj, batch, seq, sec_of)
    h, u = _out_proj(y.reshape(batch * seq, d_model), dil_w_out[0].astype(bf16), norm_post[0][None, :], h,
                     next_gain=norm_pre[1][None, :], y_order=MERGE_ORDER)

    lambda_init = 0.8 - 0.6 * math.exp(-0.3 * 1)
    tq, tk = 256, 512
    cos_t, sin_t = _rope_tables(seq, (1,))
    proj = _in_proj(u, diff_w_in[0].astype(bf16), cos_t, sin_t, batch, seq, [0, 1, 2, 3], [0, 0, 0, 0], [0], [1])
    y = _diff_attention(proj, diff_lambda_q1[0][None, :], diff_lambda_k1[0][None, :],
                        diff_lambda_q2[0][None, :], diff_lambda_k2[0][None, :],
                        diff_subln[0][None, :], batch, seq, lambda_init, tq, tk)
    h = _out_proj(y.reshape(batch * seq, d_model), diff_w_out[0].astype(bf16), norm_post[1][None, :], h)
    return h.reshape(batch, seq, d_model)
```
